```python
import math
import jax, jax.numpy as jnp
from jax import lax
import numpy as np

D_MODEL = 1024
BATCH = 32
SEQ = 256
DEPTH = 4
DEC_BATCH = 8
DEC_SEQ = 4096
PAST_LEN = 256

GRID_W = 64
Q_BLOCK = 128
ROPE_THETA = 10000.0
DIFF_HEADS = 4
DIFF_HEAD_DIM = 64
DIFF_V_DIM = 2 * DIFF_HEAD_DIM
DIFF_QK_WIDTH = DIFF_HEADS * 2 * DIFF_HEAD_DIM
DIFF_WIDTH = DIFF_HEADS * DIFF_V_DIM
GQA_HEADS = 8
GQA_KV_HEADS = 2
GQA_HEAD_DIM = 64
GQA_GROUP = GQA_HEADS // GQA_KV_HEADS
GQA_WIDTH = GQA_HEADS * GQA_HEAD_DIM
GQA_KV_WIDTH = GQA_KV_HEADS * GQA_HEAD_DIM
FOURIER_GROUPS = 4
FOURIER_GROUP_DIM = 128
FOURIER_WIDTH = FOURIER_GROUPS * FOURIER_GROUP_DIM
N_BRANCH = 3
BRANCH_WIDTH = 512
IN_SIZES = (DIFF_QK_WIDTH, DIFF_QK_WIDTH, DIFF_WIDTH, GQA_WIDTH, GQA_KV_WIDTH, GQA_KV_WIDTH,
            FOURIER_WIDTH, N_BRANCH * D_MODEL)
IN_COLS = sum(IN_SIZES)
PEER_HEADS = 8
PEER_KEYS = 128
PEER_EXPERTS = PEER_KEYS * PEER_KEYS
PEER_QUERY_DIM = 128
PEER_HALF = PEER_QUERY_DIM // 2
PEER_TOPK = 16
PEER_CHUNK = 128
DEEPNORM_ALPHA = (2 * DEPTH) ** 0.25
DEEPNORM_BETA = (8 * DEPTH) ** -0.25
LN_EPS = 1e-6

kernel_name = 'hybrid_diffusion_prefix_trunk_step'


def layer_norm(x, g=None, b=None):
    xf = x.astype(jnp.float32)
    mu = jnp.mean(xf, -1, keepdims=True)
    xc = xf - mu
    y = xc * lax.rsqrt(jnp.mean(xc * xc, -1, keepdims=True) + LN_EPS)
    if g is not None:
        y = y * g.astype(jnp.float32) + b.astype(jnp.float32)
    return y.astype(x.dtype)


def rms_norm(x, g):
    xf = x.astype(jnp.float32)
    y = xf * lax.rsqrt(jnp.mean(xf * xf, -1, keepdims=True) + LN_EPS) * g.astype(jnp.float32)
    return y.astype(x.dtype)


def axial_rope(n_tokens, dim):
    n_rows = n_tokens // GRID_W
    row = jnp.repeat(jnp.arange(n_rows), GRID_W).astype(jnp.float32)
    col = jnp.tile(jnp.arange(GRID_W), n_rows).astype(jnp.float32)
    n_freq = dim // 4
    freqs = ROPE_THETA ** (-jnp.arange(n_freq, dtype=jnp.float32) / n_freq)
    ang = jnp.concatenate([row[:, None] * freqs, col[:, None] * freqs], -1)
    return jnp.cos(ang), jnp.sin(ang)


def apply_rope(x, cos, sin):
    xp = x.astype(jnp.float32).reshape(x.shape[:-1] + (x.shape[-1] // 2, 2))
    xe, xo = xp[..., 0], xp[..., 1]
    c, s = cos[None, :, None, :], sin[None, :, None, :]
    out = jnp.stack([xe * c - xo * s, xe * s + xo * c], -1).reshape(x.shape)
    return out.astype(x.dtype)


def sweep_query_blocks(block_fn, q):
    b, s = q.shape[0], q.shape[1]
    nb = s // Q_BLOCK
    qb = jnp.moveaxis(q.reshape((b, nb, Q_BLOCK) + q.shape[2:]), 1, 0)
    out = jnp.moveaxis(lax.map(block_fn, qb), 0, 1)
    return out.reshape((b, s) + out.shape[3:])


def diff_attention(q, k, v, lam):
    scale = DIFF_HEAD_DIM ** -0.5
    def block(qb):
        s = jnp.einsum('bqhmd,bkhmd->bhmqk', qb, k).astype(jnp.float32) * scale
        a = jax.nn.softmax(s, axis=-1)
        w = (a[:, :, 0] - lam * a[:, :, 1]).astype(v.dtype)
        return jnp.einsum('bhqk,bkhd->bqhd', w, v)
    return sweep_query_blocks(block, q)


def gqa_attention(q, k, v):
    scale = GQA_HEAD_DIM ** -0.5
    def block(qb):
        s = jnp.einsum('bqhgd,bkhd->bhgqk', qb, k).astype(jnp.float32) * scale
        a = jax.nn.softmax(s, axis=-1).astype(v.dtype)
        return jnp.einsum('bhgqk,bkhd->bqhgd', a, v)
    return sweep_query_blocks(block, q)


def fourier_mix(z):
    f = jnp.fft.fft2(z.astype(jnp.float32), axes=(1, 3), norm='ortho')
    return jnp.real(f).astype(z.dtype)


def token_mixer(h, lp, rope, ctx_kv):
    b, s, _ = h.shape
    z = h @ lp['w_in']
    splits = np.cumsum(IN_SIZES)[:-1].tolist()
    dq, dk, dv, gq, gk, gv, fz, gz = jnp.split(z, splits, axis=-1)
    dq = dq.reshape(b, s, DIFF_HEADS, 2, DIFF_HEAD_DIM)
    dk = dk.reshape(b, s, DIFF_HEADS, 2 * DIFF_HEAD_DIM)
    dv = dv.reshape(b, s, DIFF_HEADS, DIFF_V_DIM)
    gq = rms_norm(gq.reshape(b, s, GQA_KV_HEADS, GQA_GROUP, GQA_HEAD_DIM), lp['q_norm_g'])
    gk = rms_norm(gk.reshape(b, s, GQA_KV_HEADS, GQA_HEAD_DIM), lp['k_norm_g'])
    gv = gv.reshape(b, s, GQA_KV_HEADS, GQA_HEAD_DIM)
    own_kv = (dk, dv, gk, gv)

    dk_att = dk.reshape(b, s, DIFF_HEADS, 2, DIFF_HEAD_DIM)
    gk_att = gk
    if rope is not None:
        cos, sin = rope
        dq = apply_rope(dq.reshape(b, s, 2 * DIFF_HEADS, DIFF_HEAD_DIM), cos, sin).reshape(dq.shape)
        dk_att = apply_rope(dk_att.reshape(b, s, 2 * DIFF_HEADS, DIFF_HEAD_DIM), cos, sin).reshape(dk_att.shape)
        gq = apply_rope(gq.reshape(b, s, GQA_HEADS, GQA_HEAD_DIM), cos, sin).reshape(gq.shape)
        gk_att = apply_rope(gk, cos, sin)
    dv_att, gv_att = dv, gv
    if ctx_kv is not None:
        cdk, cdv, cgk, cgv = ctx_kv
        p = cdk.shape[1]
        dk_att = jnp.concatenate([dk_att, cdk.reshape(b, p, DIFF_HEADS, 2, DIFF_HEAD_DIM)], 1)
        dv_att = jnp.concatenate([dv_att, cdv], 1)
        gk_att = jnp.concatenate([gk_att, cgk], 1)
        gv_att = jnp.concatenate([gv_att, cgv], 1)

    lam_init = lp['lam_init']
    lam_vec = lp['diff_lam'].astype(jnp.float32)
    lam = (jnp.exp(jnp.sum(lam_vec[0] * lam_vec[1])) - jnp.exp(jnp.sum(lam_vec[2] * lam_vec[3]))
           + lam_init)
    d_o = diff_attention(dq, dk_att, dv_att, lam)
    d_o = (rms_norm(d_o, lp['diff_norm_g']) * (1.0 - lam_init)).reshape(b, s, DIFF_WIDTH)
    g_o = gqa_attention(gq, gk_att, gv_att).reshape(b, s, GQA_WIDTH)
    f_o = fourier_mix(fz.reshape(b, s, FOURIER_GROUPS, FOURIER_GROUP_DIM)).reshape(b, s, FOURIER_WIDTH)

    branches = jnp.stack([d_o, g_o, f_o], axis=2)
    y = jnp.einsum('bsnc,ncd->bsnd', branches, lp['w_branch'])
    gates = jax.nn.sigmoid(gz.reshape(b, s, N_BRANCH, D_MODEL))
    out = jnp.sum(gates * y, axis=2) @ lp['w_out']
    return out, own_kv


def peer_ffn(h, lp):
    b, s, d = h.shape
    t = b * s
    x = h.reshape(t, d)
    q = (x @ lp['peer_wq']).reshape(t, PEER_HEADS, 2, PEER_HALF)
    sc = jnp.einsum('thpd,hpnd->thpn', q, lp['peer_subkeys']).astype(jnp.float32)
    sv, si = lax.top_k(sc, PEER_TOPK)
    cand = (sv[:, :, 0, :, None] + sv[:, :, 1, None, :]).reshape(t, PEER_HEADS, PEER_TOPK * PEER_TOPK)
    cidx = (si[:, :, 0, :, None] * PEER_KEYS + si[:, :, 1, None, :]).reshape(t, PEER_HEADS, PEER_TOPK * PEER_TOPK)
    top_v, top_p = lax.top_k(cand, PEER_TOPK)
    experts = jnp.take_along_axis(cidx, top_p, axis=-1)
    gates = jax.nn.softmax(top_v, axis=-1).astype(h.dtype)
    u_tab, v_tab = lp['peer_u'], lp['peer_v']

    def chunk(args):
        xc, ec, gc = args
        act = jax.nn.gelu(jnp.einsum('cd,chkd->chk', xc, u_tab[ec]), approximate=False)
        return jnp.einsum('chk,chkd->cd', gc * act, v_tab[ec])

    n = t // PEER_CHUNK
    out = lax.map(chunk, (x.reshape(n, PEER_CHUNK, d),
                          experts.reshape(n, PEER_CHUNK, PEER_HEADS, PEER_TOPK),
                          gates.reshape(n, PEER_CHUNK, PEER_HEADS, PEER_TOPK)))
    return out.reshape(b, s, d)


def trunk_layer(x, mod, lp, rope, ctx_kv):
    sh1, sc1, g1, sh2, sc2, g2 = jnp.split(mod, 6, axis=-1)
    h = layer_norm(x) * (1.0 + sc1) + sh1
    mix, own_kv = token_mixer(h, lp, rope, ctx_kv)
    x = layer_norm(DEEPNORM_ALPHA * x + g1 * mix, lp['ln_g'][0], lp['ln_b'][0])
    h = layer_norm(x) * (1.0 + sc2) + sh2
    x = layer_norm(DEEPNORM_ALPHA * x + g2 * peer_ffn(h, lp), lp['ln_g'][1], lp['ln_b'][1])
    return x, own_kv


def setup_inputs(seed: int = 0) -> dict:
    key = jax.random.key(seed)
    ks = jax.random.split(key, 24)
    def nrm(k, shape, scale):
        return jax.random.normal(k, shape, jnp.float32) * scale
    d = D_MODEL
    return {
        'x_prompt': nrm(ks[0], (BATCH, SEQ, d), 1.0),
        'x_sample': nrm(ks[1], (DEC_BATCH, DEC_SEQ, d), 1.0),
        'cache_diff_k': nrm(ks[2], (DEC_BATCH, DEPTH, PAST_LEN, DIFF_HEADS, 2 * DIFF_HEAD_DIM), 1.0),
        'cache_diff_v': nrm(ks[3], (DEC_BATCH, DEPTH, PAST_LEN, DIFF_HEADS, DIFF_V_DIM), 1.0),
        'cache_gqa_k': nrm(ks[4], (DEC_BATCH, DEPTH, PAST_LEN, GQA_KV_HEADS, GQA_HEAD_DIM), 1.0),
        'cache_gqa_v': nrm(ks[5], (DEC_BATCH, DEPTH, PAST_LEN, GQA_KV_HEADS, GQA_HEAD_DIM), 1.0),
        'c': nrm(ks[6], (DEC_BATCH, d), 1.0),
        'c_ctx': nrm(ks[7], (d,), 1.0),
        'w_mod': nrm(ks[8], (DEPTH, d, 6 * d), 0.5 * d ** -0.5),
        'b_mod': nrm(ks[9], (DEPTH, 6 * d), 0.01),
        'w_in': nrm(ks[10], (DEPTH, d, IN_COLS), d ** -0.5),
        'diff_lam': nrm(ks[11], (DEPTH, 4, DIFF_HEAD_DIM), 0.1),
        'diff_norm_g': 1.0 + nrm(ks[12], (DEPTH, DIFF_V_DIM), 0.01),
        'q_norm_g': 1.0 + nrm(ks[13], (DEPTH, GQA_HEAD_DIM), 0.01),
        'k_norm_g': 1.0 + nrm(ks[14], (DEPTH, GQA_HEAD_DIM), 0.01),
        'w_branch': nrm(ks[15], (DEPTH, N_BRANCH, BRANCH_WIDTH, d), DEEPNORM_BETA * BRANCH_WIDTH ** -0.5),
        'w_out': nrm(ks[16], (DEPTH, d, d), DEEPNORM_BETA * d ** -0.5),
        'ln_g': 1.0 + nrm(ks[17], (DEPTH, 2, d), 0.01),
        'ln_b': nrm(ks[18], (DEPTH, 2, d), 0.01),
        'peer_wq': nrm(ks[19], (DEPTH, d, PEER_HEADS * PEER_QUERY_DIM), d ** -0.5),
        'peer_subkeys': nrm(ks[20], (DEPTH, PEER_HEADS, 2, PEER_KEYS, PEER_HALF), PEER_HALF ** -0.5),
        'peer_u': nrm(ks[21], (DEPTH, PEER_EXPERTS, d), d ** -0.5),
        'peer_v': nrm(ks[22], (DEPTH, PEER_EXPERTS, d), DEEPNORM_BETA),
    }


def reference(x_prompt, x_sample, cache_diff_k, cache_diff_v, cache_gqa_k, cache_gqa_v, c, c_ctx,
              w_mod, b_mod, w_in, diff_lam, diff_norm_g, q_norm_g, k_norm_g, w_branch, w_out,
              ln_g, ln_b, peer_wq, peer_subkeys, peer_u, peer_v):
    def layer_params(l):
        return {'w_in': w_in[l], 'diff_lam': diff_lam[l], 'diff_norm_g': diff_norm_g[l],
                'q_norm_g': q_norm_g[l], 'k_norm_g': k_norm_g[l], 'w_branch': w_branch[l],
                'w_out': w_out[l], 'ln_g': ln_g[l], 'ln_b': ln_b[l], 'peer_wq': peer_wq[l],
                'peer_subkeys': peer_subkeys[l], 'peer_u': peer_u[l], 'peer_v': peer_v[l],
                'lam_init': 0.8 - 0.6 * math.exp(-0.3 * l)}

    xp = x_prompt
    dk_list, dv_list, gk_list, gv_list = [], [], [], []
    for l in range(DEPTH):
        lp = layer_params(l)
        mod = (jax.nn.silu(c_ctx) @ w_mod[l] + b_mod[l])[None, None, :]
        xp, (dk, dv, gk, gv) = trunk_layer(xp, mod, lp, None, None)
        dk_list.append(dk); dv_list.append(dv); gk_list.append(gk); gv_list.append(gv)
    new_diff_k = jnp.stack(dk_list, axis=1)
    new_diff_v = jnp.stack(dv_list, axis=1)
    new_gqa_k = jnp.stack(gk_list, axis=1)
    new_gqa_v = jnp.stack(gv_list, axis=1)

    rope = axial_rope(x_sample.shape[1], DIFF_HEAD_DIM)
    xs = x_sample
    for l in range(DEPTH):
        lp = layer_params(l)
        mod = (jax.nn.silu(c) @ w_mod[l] + b_mod[l])[:, None, :]
        ctx_kv = (cache_diff_k[:, l], cache_diff_v[:, l], cache_gqa_k[:, l], cache_gqa_v[:, l])
        xs, _ = trunk_layer(xs, mod, lp, rope, ctx_kv)

    return (xp, xs, new_diff_k, new_diff_v, new_gqa_k, new_gqa_v)
```

```python
import functools
import math

import numpy as np
import jax
import jax.numpy as jnp
from jax import lax
from jax.experimental import pallas as pl
from jax.experimental.pallas import tpu as pltpu

F32 = jnp.float32
BF16 = jnp.bfloat16

D_MODEL = 1024
GRID_W = 64
ROPE_THETA = 10000.0
DIFF_HEADS = 4
DIFF_HEAD_DIM = 64
GQA_KV_HEADS = 2
GQA_HEAD_DIM = 64
FOURIER_GROUP_DIM = 128
FOURIER_GROUPS = 4
N_BRANCH = 3
BRANCH_WIDTH = 512
C_DQ, C_DK, C_DV, C_GQ, C_GK, C_GV, C_FZ, C_GZ, C_END = 0, 512, 1024, 1536, 2048, 2176, 2304, 2816, 5888
PEER_HEADS = 8
PEER_KEYS = 128
PEER_EXPERTS = PEER_KEYS * PEER_KEYS
PEER_TOPK = 16
DEPTH_FOR_NORM = 4
DEEPNORM_ALPHA = (2 * DEPTH_FOR_NORM) ** 0.25
LN_EPS = 1e-6
ATTN_SCALE = 0.125
SQRT_HALF = math.sqrt(0.5)

LANES = 128
VMEM_LIMIT = 56 * 1024 * 1024
NEG_INF = float("-inf")


def _cparams(sem):
    return pltpu.CompilerParams(dimension_semantics=sem, vmem_limit_bytes=VMEM_LIMIT)


def _ln(x):
    mu = jnp.mean(x, -1, keepdims=True)
    xc = x - mu
    return xc * lax.rsqrt(jnp.mean(xc * xc, -1, keepdims=True) + LN_EPS)


def _dot(a, b):
    return jnp.dot(a, b, preferred_element_type=F32)


def _dot_nt(a, b):
    return lax.dot_general(a, b, (((1,), (1,)), ((), ())), preferred_element_type=F32)


def _mod_kernel(c_ref, w_ref, b_ref, o_ref):
    c = c_ref[...]
    a = c * jax.nn.sigmoid(c)
    o_ref[0] = jnp.dot(a, w_ref[0], preferred_element_type=F32, precision=lax.Precision.HIGHEST) + b_ref[0]


def _mod_call(cs, w_mod, b_mod):
    depth, d, n = w_mod.shape
    rows = cs.shape[0]
    nt = 1536
    return pl.pallas_call(
        _mod_kernel,
        grid=(depth, n // nt),
        in_specs=[pl.BlockSpec((rows, d), lambda l, j: (0, 0)),
                  pl.BlockSpec((1, d, nt), lambda l, j: (l, 0, j)),
                  pl.BlockSpec((1, 1, nt), lambda l, j: (l, 0, j))],
        out_specs=pl.BlockSpec((1, rows, nt), lambda l, j: (l, 0, j)),
        out_shape=jax.ShapeDtypeStruct((depth, rows, n), F32),
        compiler_params=_cparams(("arbitrary", "arbitrary")),
        name="mod_proj",
    )(cs, w_mod, b_mod.reshape(depth, 1, n))


def _rms64(y, bd, g):
    y2 = y * y
    hi = y2.astype(BF16)
    lo = (y2 - hi.astype(F32)).astype(BF16)
    ms = _dot(hi, bd) + _dot(lo, bd)
    return y * lax.rsqrt(ms + LN_EPS) * g


def _rope(y, cos, sin_signed):
    w = y.shape[1]
    reps = w // LANES
    c = jnp.concatenate([cos] * reps, axis=1) if reps > 1 else cos
    s = jnp.concatenate([sin_signed] * reps, axis=1) if reps > 1 else sin_signed
    lane = lax.broadcasted_iota(jnp.int32, y.shape, 1)
    even = (lane & 1) == 0
    partner = jnp.where(even, pltpu.roll(y, w - 1, 1), pltpu.roll(y, 1, 1))
    return y * c + partner * s


def _in_proj_kernel(*refs, rope, emit_kv):
    x_ref, mod_ref, w_ref, qg_ref, kg_ref, bd_ref = refs[:6]
    pos = 6
    if rope:
        cos_ref, sin_ref = refs[6:8]
        pos = 8
    dq_ref, dk_ref, dv_ref, gq_ref, gk_ref, gv_ref, fz_ref, gate_ref = refs[pos:pos + 8]
    pos += 8
    if emit_kv:
        ndk_ref, ndv_ref, ngk_ref, ngv_ref = refs[pos:pos + 4]

    mod = mod_ref[0]
    sh1 = mod[:, 0:D_MODEL]
    sc1 = mod[:, D_MODEL:2 * D_MODEL]
    h = (_ln(x_ref[...]) * (1.0 + sc1) + sh1).astype(BF16)

    def seg(lo, hi):
        return _dot(h, w_ref[:, lo:hi])

    if rope:
        cos = cos_ref[...]
        sin = sin_ref[...]
        rot = lambda y: _rope(y, cos, sin)
    else:
        rot = lambda y: y

    dq_ref[...] = (rot(seg(C_DQ, C_DK)) * ATTN_SCALE).astype(BF16)
    dk = seg(C_DK, C_DV)
    dk_ref[...] = rot(dk).astype(BF16)
    dv = seg(C_DV, C_GQ)
    dv_ref[...] = dv.astype(BF16)
    gq = _rms64(seg(C_GQ, C_GK), bd_ref[...], qg_ref[...])
    gq_ref[...] = (rot(gq) * ATTN_SCALE).astype(BF16)
    gk = _rms64(seg(C_GK, C_GV), bd_ref[0:LANES, 0:LANES], kg_ref[...])
    gk_ref[...] = rot(gk).astype(BF16)
    gv = seg(C_GV, C_FZ)
    gv_ref[...] = gv.astype(BF16)
    fz_ref[...] = seg(C_FZ, C_GZ).astype(BF16)
    for n in range(N_BRANCH):
        lo = C_GZ + n * D_MODEL
        gate_ref[:, n * D_MODEL:(n + 1) * D_MODEL] = jax.nn.sigmoid(seg(lo, lo + D_MODEL))
    if emit_kv:
        ndk_ref[...] = dk
        ndv_ref[...] = dv
        ngk_ref[...] = gk
        ngv_ref[...] = gv


def _in_proj_call(x, mod3, mod_idx, w_in, qg, kg, bd, rope_tabs, seq, tt, emit_kv):
    t, d = x.shape
    rope = rope_tabs is not None
    nblk_seq = seq // tt
    full = lambda shape: pl.BlockSpec(shape, lambda i: (0,) * len(shape))
    tok = lambda w: pl.BlockSpec((tt, w), lambda i: (i, 0))
    in_specs = [tok(d),
                pl.BlockSpec((1, 1, mod3.shape[2]), lambda i: (mod_idx(i), 0, 0)),
                pl.BlockSpec(w_in.shape, lambda i: (0, 0), pipeline_mode=pl.Buffered(1)),
                full(qg.shape), full(kg.shape), full(bd.shape)]
    args = [x, mod3, w_in, qg, kg, bd]
    if rope:
        in_specs += [pl.BlockSpec((tt, LANES), lambda i: (i % nblk_seq, 0))] * 2
        args += list(rope_tabs)
    widths = [512, 512, 512, 512, 128, 128, 512]
    out_specs = [tok(w) for w in widths] + [tok(N_BRANCH * d)]
    out_shape = [jax.ShapeDtypeStruct((t, w), BF16) for w in widths] + [jax.ShapeDtypeStruct((t, N_BRANCH * d), F32)]
    if emit_kv:
        for w in (512, 512, 128, 128):
            out_specs.append(tok(w))
            out_shape.append(jax.ShapeDtypeStruct((t, w), F32))
    return pl.pallas_call(
        functools.partial(_in_proj_kernel, rope=rope, emit_kv=emit_kv),
        grid=(t // tt,),
        in_specs=in_specs,
        out_specs=out_specs,
        out_shape=out_shape,
        compiler_params=_cparams(("arbitrary",)),
        name="in_proj",
    )(*args)


def _attn_kernel(dq_ref, k_ref, v_ref, gq_ref, gk_ref, gv_ref, lam_ref, dng_ref, do_ref, go_ref, *, lam_init):
    tq = dq_ref.shape[1]
    lane = lax.broadcasted_iota(jnp.int32, (tq, LANES), 1)
    is_lo = lane < DIFF_HEAD_DIM
    m_lo = is_lo.astype(F32)
    m_hi = 1.0 - m_lo

    def split_halves(q):
        qf = q.astype(F32)
        return jnp.concatenate([(qf * m_lo).astype(BF16), (qf * m_hi).astype(BF16)], axis=0)

    lv = lam_ref[...]
    lam = (jnp.exp(jnp.sum(lv[0:1] * lv[1:2], keepdims=True))
           - jnp.exp(jnp.sum(lv[2:3] * lv[3:4], keepdims=True)) + lam_init)
    dng = dng_ref[...]

    for h in range(DIFF_HEADS):
        sl = slice(h * LANES, (h + 1) * LANES)
        s = _dot_nt(split_halves(dq_ref[0, :, sl]), k_ref[0, :, sl])
        e = jnp.exp(s - jnp.max(s, -1, keepdims=True))
        a = e / jnp.sum(e, -1, keepdims=True)
        w = (a[:tq] - lam * a[tq:]).astype(BF16)
        o = _dot(w, v_ref[0, :, sl])
        o = o * lax.rsqrt(jnp.mean(o * o, -1, keepdims=True) + LN_EPS) * dng * (1.0 - lam_init)
        do_ref[0, :, sl] = o.astype(BF16)

    for hk in range(GQA_KV_HEADS):
        ksl = slice(hk * LANES, (hk + 1) * LANES)
        for m in range(2):
            c0 = (hk * 2 + m) * LANES
            sl = slice(c0, c0 + LANES)
            s = _dot_nt(split_halves(gq_ref[0, :, sl]), gk_ref[0, :, ksl])
            e = jnp.exp(s - jnp.max(s, -1, keepdims=True))
            l = jnp.sum(e, -1, keepdims=True)
            o = _dot(e.astype(BF16), gv_ref[0, :, ksl]) / l
            go_ref[0, :, sl] = jnp.where(is_lo, o[:tq], o[tq:]).astype(BF16)


def _attn_call(dq, k, v, gq, gk, gv, lamv, dng, lam_init, tq):
    b, s, w = dq.shape
    sk = k.shape[1]
    qspec = pl.BlockSpec((1, tq, w), lambda bi, i: (bi, i, 0))
    kvspec = lambda width: pl.BlockSpec((1, sk, width), lambda bi, i: (bi, 0, 0))
    full = lambda shape: pl.BlockSpec(shape, lambda bi, i: (0,) * len(shape))
    return pl.pallas_call(
        functools.partial(_attn_kernel, lam_init=lam_init),
        grid=(b, s // tq),
        in_specs=[qspec, kvspec(512), kvspec(512), qspec, kvspec(256), kvspec(256),
                  full(lamv.shape), full(dng.shape)],
        out_specs=[qspec, qspec],
        out_shape=[jax.ShapeDtypeStruct((b, s, w), BF16)] * 2,
        compiler_params=_cparams(("arbitrary", "arbitrary")),
        name="attention",
    )(dq, k, v, gq, gk, gv, lamv, dng)


def _fourier_kernel(wc_ref, ws_ref, z_ref, cc_ref, sc_ref, o_ref, *, scale):
    z = z_ref[0]
    p = _dot(wc_ref[...], z).astype(BF16)
    q = _dot(ws_ref[...], z).astype(BF16)
    o_ref[0] = ((_dot(p, cc_ref[...]) - _dot(q, sc_ref[...])) * scale).astype(BF16)


def _fourier_call(z, wc, ws, cc, sc, tm):
    b, s, w = z.shape
    scale = 1.0 / math.sqrt(s * FOURIER_GROUP_DIM)
    return pl.pallas_call(
        functools.partial(_fourier_kernel, scale=scale),
        grid=(s // tm, b),
        in_specs=[pl.BlockSpec((tm, s), lambda i, bi: (i, 0)),
                  pl.BlockSpec((tm, s), lambda i, bi: (i, 0)),
                  pl.BlockSpec((1, s, w), lambda i, bi: (bi, 0, 0)),
                  pl.BlockSpec((w, w), lambda i, bi: (0, 0)),
                  pl.BlockSpec((w, w), lambda i, bi: (0, 0))],
        out_specs=pl.BlockSpec((1, tm, w), lambda i, bi: (bi, i, 0)),
        out_shape=jax.ShapeDtypeStruct((b, s, w), BF16),
        compiler_params=_cparams(("arbitrary", "arbitrary")),
        name="fourier",
    )(wc, ws, z, cc, sc)


def _dft_tables(n):
    k = jnp.arange(n, dtype=jnp.int32)
    m = (k[:, None] * k[None, :]) % n
    ang = m.astype(F32) * (2.0 * math.pi / n)
    return jnp.cos(ang), jnp.sin(ang)


def _merge_kernel(d_ref, g_ref, f_ref, gate_ref, x_ref, mod_ref, wb_ref, wo_ref, lng_ref, lnb_ref, o_ref):
    d = D_MODEL
    m = (gate_ref[:, 0:d] * _dot(d_ref[...], wb_ref[0])
         + gate_ref[:, d:2 * d] * _dot(g_ref[...], wb_ref[1])
         + gate_ref[:, 2 * d:3 * d] * _dot(f_ref[...], wb_ref[2]))
    mix = _dot(m.astype(BF16), wo_ref[...])
    g1 = mod_ref[0][:, 2 * d:3 * d]
    u = DEEPNORM_ALPHA * x_ref[...] + g1 * mix
    o_ref[...] = _ln(u) * lng_ref[...] + lnb_ref[...]


def _merge_call(d_o, g_o, f_o, gates, x, mod3, mod_idx, wb, wo, lng, lnb, tt):
    t, d = x.shape
    full = lambda shape: pl.BlockSpec(shape, lambda i: (0,) * len(shape))
    tok = lambda w: pl.BlockSpec((tt, w), lambda i: (i, 0))
    return pl.pallas_call(
        _merge_kernel,
        grid=(t // tt,),
        in_specs=[tok(512), tok(512), tok(512), tok(N_BRANCH * d), tok(d),
                  pl.BlockSpec((1, 1, mod3.shape[2]), lambda i: (mod_idx(i), 0, 0)),
                  full(wb.shape), full(wo.shape), full(lng.shape), full(lnb.shape)],
        out_specs=tok(d),
        out_shape=jax.ShapeDtypeStruct((t, d), F32),
        compiler_params=_cparams(("arbitrary",)),
        name="merge",
    )(d_o, g_o, f_o, gates, x, mod3, wb, wo, lng, lnb)


N_TOP = PEER_TOPK + 1
PAD_TOP = 24
N_CAND = PAD_TOP + 7 * 8 + (PAD_TOP - 8)


def _top_values(x, n):
    vals = []
    for _ in range(n):
        m = jnp.max(x, axis=0, keepdims=True)
        vals.append(m)
        x = jnp.where(x >= m, NEG_INF, x)
    return vals


def _route_head(s0, s1, av_ref, bv_ref, cand_ref):
    a = _top_values(s0, N_TOP)
    b = _top_values(s1, N_TOP)
    for k in range(N_TOP):
        av_ref[k:k + 1, :] = a[k]
        bv_ref[k:k + 1, :] = b[k]
    av = av_ref[...]
    bv = bv_ref[...]
    cand_ref[0:PAD_TOP, :] = a[0] + bv
    for k in range(1, 8):
        cand_ref[PAD_TOP + 8 * (k - 1):PAD_TOP + 8 * k, :] = a[k] + bv[0:8]
    cand_ref[PAD_TOP + 56:N_CAND, :] = av[8:PAD_TOP] + b[0]
    cand = cand_ref[...]
    c = _top_values(cand, N_TOP)
    tau = 0.5 * (c[PEER_TOPK - 1] + c[PEER_TOPK])
    z = jnp.sum(jnp.where(cand >= tau, jnp.exp(cand - c[0]), 0.0), axis=0, keepdims=True)
    w0 = jnp.exp(s0 - a[0]) / z
    w1 = jnp.exp(s1 - b[0])
    return -s0, s1 - tau, w0, w1


def _route_kernel(x_ref, mod_ref, wq_ref, sk_ref, hb_ref, ns0_ref, a1_ref, w0_ref, w1_ref,
                  av_ref, bv_ref, cand_ref):
    d = D_MODEL
    tt = x_ref.shape[0]
    mod = mod_ref[0]
    sh2 = mod[:, 3 * d:4 * d]
    sc2 = mod[:, 4 * d:5 * d]
    hb = (_ln(x_ref[...]) * (1.0 + sc2) + sh2).astype(BF16)
    hb_ref[...] = hb
    q = _dot(hb, wq_ref[...])
    lane = lax.broadcasted_iota(jnp.int32, (tt, LANES), 1)
    m_lo = (lane < PEER_KEYS // 2).astype(F32)
    m_hi = 1.0 - m_lo
    av_ref[...] = jnp.full(av_ref.shape, NEG_INF, F32)
    bv_ref[...] = jnp.full(bv_ref.shape, NEG_INF, F32)
    for h in range(PEER_HEADS):
        qh = q[:, h * LANES:(h + 1) * LANES]
        sk = sk_ref[h]
        s0 = _dot_nt(sk, (qh * m_lo).astype(BF16))
        s1 = _dot_nt(sk, (qh * m_hi).astype(BF16))
        ns0, a1, w0, w1 = _route_head(s0, s1, av_ref, bv_ref, cand_ref)
        ns0_ref[h] = ns0
        a1_ref[h] = a1
        w0_ref[h] = w0
        w1_ref[h] = w1


def _route_call(x, mod3, mod_idx, wq, skc, tt):
    t, d = x.shape
    full = lambda shape: pl.BlockSpec(shape, lambda i: (0,) * len(shape))
    rspec = pl.BlockSpec((PEER_HEADS, PEER_KEYS, tt), lambda i: (0, 0, i))
    rshape = jax.ShapeDtypeStruct((PEER_HEADS, PEER_KEYS, t), F32)
    return pl.pallas_call(
        _route_kernel,
        grid=(t // tt,),
        in_specs=[pl.BlockSpec((tt, d), lambda i: (i, 0)),
                  pl.BlockSpec((1, 1, mod3.shape[2]), lambda i: (mod_idx(i), 0, 0)),
                  full(wq.shape), full(skc.shape)],
        out_specs=[pl.BlockSpec((tt, d), lambda i: (i, 0)), rspec, rspec, rspec, rspec],
        out_shape=[jax.ShapeDtypeStruct((t, d), BF16), rshape, rshape, rshape, rshape],
        scratch_shapes=[pltpu.VMEM((PAD_TOP, tt), F32), pltpu.VMEM((PAD_TOP, tt), F32),
                        pltpu.VMEM((N_CAND, tt), F32)],
        compiler_params=_cparams(("arbitrary",)),
        name="peer_route",
    )(x, mod3, wq, skc)


def _peer_kernel(hb_ref, ns0_ref, a1_ref, w0_ref, w1_ref, u_ref, vt_ref, x_ref, mod_ref, lng_ref, lnb_ref,
                 o_ref, acc_ref, ga_ref, *, ni):
    c = pl.program_id(1)

    @pl.when(c == 0)
    def _():
        acc_ref[...] = jnp.zeros_like(acc_ref)

    hb = hb_ref[...]
    for ii in range(ni):
        i = c * ni + ii
        g = None
        for h in range(PEER_HEADS):
            nrow = ns0_ref[h, pl.ds(i, 1), :]
            wrow = w0_ref[h, pl.ds(i, 1), :]
            term = jnp.where(a1_ref[h] >= nrow, w1_ref[h] * wrow, 0.0)
            g = term if g is None else g + term
        a = _dot_nt(u_ref[ii * PEER_KEYS:(ii + 1) * PEER_KEYS, :], hb)
        act = 0.5 * a * (1.0 + lax.erf(a * SQRT_HALF))
        ga_ref[ii * PEER_KEYS:(ii + 1) * PEER_KEYS, :] = (g * act).astype(BF16)
    acc_ref[...] += _dot(vt_ref[...], ga_ref[...])

    @pl.when(c == pl.num_programs(1) - 1)
    def _():
        d = D_MODEL
        g2 = mod_ref[0][:, 5 * d:6 * d]
        u = DEEPNORM_ALPHA * x_ref[...] + g2 * acc_ref[...].T
        o_ref[...] = _ln(u) * lng_ref[...] + lnb_ref[...]


def _peer_call(hb, ns0, a1, w0, w1, u_tab, vt_tab, x, mod3, mod_idx, lng, lnb, tt, ec):
    t, d = x.shape
    ne = u_tab.shape[0]
    ni = ec // PEER_KEYS
    full = lambda shape: pl.BlockSpec(shape, lambda i, c: (0,) * len(shape))
    rspec = pl.BlockSpec((PEER_HEADS, PEER_KEYS, tt), lambda i, c: (0, 0, i))
    return pl.pallas_call(
        functools.partial(_peer_kernel, ni=ni),
        grid=(t // tt, ne // ec),
        in_specs=[pl.BlockSpec((tt, d), lambda i, c: (i, 0)), rspec, rspec, rspec, rspec,
                  pl.BlockSpec((ec, d), lambda i, c: (c, 0)),
                  pl.BlockSpec((d, ec), lambda i, c: (0, c)),
                  pl.BlockSpec((tt, d), lambda i, c: (i, 0)),
                  pl.BlockSpec((1, 1, mod3.shape[2]), lambda i, c: (mod_idx(i), 0, 0)),
                  full(lng.shape), full(lnb.shape)],
        out_specs=pl.BlockSpec((tt, d), lambda i, c: (i, 0)),
        out_shape=jax.ShapeDtypeStruct((t, d), F32),
        scratch_shapes=[pltpu.VMEM((d, tt), F32), pltpu.VMEM((ec, tt), BF16)],
        compiler_params=_cparams(("arbitrary", "arbitrary")),
        name="peer_experts",
    )(hb, ns0, a1, w0, w1, u_tab, vt_tab, x, mod3, lng, lnb)


def _pick_tile(n, want):
    t = min(n, want)
    while n % t:
        t //= 2
    return t


def _rope_tables(seq):
    n_rows = seq // GRID_W
    row = jnp.repeat(jnp.arange(n_rows), GRID_W).astype(F32)
    col = jnp.tile(jnp.arange(GRID_W), n_rows).astype(F32)
    n_freq = DIFF_HEAD_DIM // 4
    freqs = ROPE_THETA ** (-jnp.arange(n_freq, dtype=F32) / n_freq)
    ang = jnp.concatenate([row[:, None] * freqs, col[:, None] * freqs], -1)
    cos = jnp.repeat(jnp.cos(ang), 2, axis=-1)
    sin = jnp.repeat(jnp.sin(ang), 2, axis=-1)
    sign = jnp.where(jnp.arange(DIFF_HEAD_DIM) % 2 == 0, -1.0, 1.0).astype(F32)
    reps = LANES // DIFF_HEAD_DIM
    return jnp.tile(cos, (1, reps)), jnp.tile(sin * sign, (1, reps))


def _dup_heads(a):
    b, s, _ = a.shape
    a4 = a.reshape(b, s, GQA_KV_HEADS, 1, GQA_HEAD_DIM)
    return jnp.broadcast_to(a4, (b, s, GQA_KV_HEADS, 2, GQA_HEAD_DIM)).reshape(b, s, 4 * GQA_HEAD_DIM)


def _layer(x, batch, seq, mod3, mod_idx_fn, lw, rope_tabs, ctx_kv, dft, emit_kv):
    t = batch * seq
    tile = lambda want: _pick_tile(seq if rope_tabs is not None else t, want)
    tt = tile(256)
    mod_idx = lambda i: mod_idx_fn(i, max(seq // tt, 1))
    outs = _in_proj_call(x, mod3, mod_idx, lw["w_in"], lw["qg"], lw["kg"], lw["bd"], rope_tabs, seq, tt, emit_kv)
    dq, dk, dv, gq, gk, gv, fz, gates = outs[:8]
    r3 = lambda a: a.reshape(batch, seq, a.shape[-1])
    k_att, v_att, gk_att, gv_att = r3(dk), r3(dv), r3(gk), r3(gv)
    if ctx_kv is not None:
        cdk, cdv, cgk, cgv = ctx_kv
        k_att = jnp.concatenate([k_att, cdk], axis=1)
        v_att = jnp.concatenate([v_att, cdv], axis=1)
        gk_att = jnp.concatenate([gk_att, cgk], axis=1)
        gv_att = jnp.concatenate([gv_att, cgv], axis=1)
    tq = _pick_tile(seq, 128 if k_att.shape[1] > 1024 else 256)
    d_o, g_o = _attn_call(r3(dq), k_att, v_att, r3(gq), _dup_heads(gk_att), _dup_heads(gv_att),
                          lw["diff_lam"], lw["dng"], lw["lam_init"], tq)
    f_o = _fourier_call(r3(fz), dft[0], dft[1], lw["cc"], lw["sc"], _pick_tile(seq, 512))
    flat = lambda a: a.reshape(t, a.shape[-1])
    tm = tile(512)
    mod_idx_m = lambda i: mod_idx_fn(i, max(seq // tm, 1))
    x1 = _merge_call(flat(d_o), flat(g_o), flat(f_o), gates, x, mod3, mod_idx_m, lw["wb"], lw["wo"],
                     lw["lng0"], lw["lnb0"], tm)
    tr = tile(256)
    mod_idx_r = lambda i: mod_idx_fn(i, max(seq // tr, 1))
    hb, ns0, a1, w0, w1 = _route_call(x1, mod3, mod_idx_r, lw["wq"], lw["skc"], tr)
    x2 = _peer_call(hb, ns0, a1, w0, w1, lw["u"], lw["vt"], x1, mod3, mod_idx_m, lw["lng1"], lw["lnb1"], tm, 1024)
    return x2, outs[8:]


def kernel(x_prompt, x_sample, cache_diff_k, cache_diff_v, cache_gqa_k, cache_gqa_v, c, c_ctx, w_mod, b_mod, w_in,
           diff_lam, diff_norm_g, q_norm_g, k_norm_g, w_branch, w_out, ln_g, ln_b, peer_wq, peer_subkeys, peer_u,
           peer_v):
    depth = w_in.shape[0]
    batch, seq, d = x_prompt.shape
    dbatch, dseq, _ = x_sample.shape
    past = cache_diff_k.shape[2]

    rows = ((dbatch + 1 + 7) // 8) * 8
    cs = jnp.zeros((rows, d), F32).at[:dbatch].set(c).at[dbatch].set(c_ctx)
    mod_all = _mod_call(cs, w_mod, b_mod)

    w_in_b = w_in.astype(BF16)
    wb_b = w_branch.astype(BF16)
    wo_b = w_out.astype(BF16)
    wq_b = peer_wq.astype(BF16)
    u_b = peer_u.astype(BF16)
    vt_b = jnp.swapaxes(peer_v, 1, 2).astype(BF16)
    skc = jnp.transpose(peer_subkeys, (0, 1, 3, 2, 4)).reshape(depth, PEER_HEADS, PEER_KEYS, PEER_KEYS).astype(BF16)
    bd = jnp.asarray(np.kron(np.eye(8, dtype=np.float32), np.full((64, 64), 1.0 / 64, np.float32)), BF16)
    ck, sk_ = _dft_tables(FOURIER_GROUP_DIM)
    eye_g = jnp.eye(FOURIER_GROUPS, dtype=F32)
    cc = jnp.kron(eye_g, ck).astype(BF16)
    sc = jnp.kron(eye_g, sk_).astype(BF16)
    dft_ctx = tuple(a.astype(BF16) for a in _dft_tables(seq))
    dft_lat = tuple(a.astype(BF16) for a in _dft_tables(dseq))
    rope_tabs = _rope_tables(dseq)

    def layer_weights(l):
        return {"w_in": w_in_b[l], "qg": jnp.tile(q_norm_g[l], 8)[None, :], "kg": jnp.tile(k_norm_g[l], 2)[None, :],
                "bd": bd, "diff_lam": diff_lam[l], "dng": diff_norm_g[l][None, :],
                "lam_init": 0.8 - 0.6 * math.exp(-0.3 * l), "cc": cc, "sc": sc, "wb": wb_b[l], "wo": wo_b[l],
                "lng0": ln_g[l, 0][None, :], "lnb0": ln_b[l, 0][None, :], "lng1": ln_g[l, 1][None, :],
                "lnb1": ln_b[l, 1][None, :], "wq": wq_b[l], "skc": skc[l], "u": u_b[l], "vt": vt_b[l]}

    xp = x_prompt.reshape(batch * seq, d)
    ctx_idx = lambda i, bps: dbatch
    kv = [[], [], [], []]
    for l in range(depth):
        mod3 = mod_all[l].reshape(rows, 1, 6 * d)
        xp, own = _layer(xp, batch, seq, mod3, ctx_idx, layer_weights(l), None, None, dft_ctx, True)
        for lst, a in zip(kv, own):
            lst.append(a)
    new_diff_k = jnp.stack(kv[0], 0).reshape(depth, batch, seq, DIFF_HEADS, 2 * DIFF_HEAD_DIM).swapaxes(0, 1)
    new_diff_v = jnp.stack(kv[1], 0).reshape(depth, batch, seq, DIFF_HEADS, 2 * DIFF_HEAD_DIM).swapaxes(0, 1)
    new_gqa_k = jnp.stack(kv[2], 0).reshape(depth, batch, seq, GQA_KV_HEADS, GQA_HEAD_DIM).swapaxes(0, 1)
    new_gqa_v = jnp.stack(kv[3], 0).reshape(depth, batch, seq, GQA_KV_HEADS, GQA_HEAD_DIM).swapaxes(0, 1)

    xs = x_sample.reshape(dbatch * dseq, d)
    lat_idx = lambda i, bps: i // bps
    for l in range(depth):
        mod3 = mod_all[l].reshape(rows, 1, 6 * d)
        ctx_kv = (cache_diff_k[:, l].reshape(dbatch, past, 512).astype(BF16),
                  cache_diff_v[:, l].reshape(dbatch, past, 512).astype(BF16),
                  cache_gqa_k[:, l].reshape(dbatch, past, 128).astype(BF16),
                  cache_gqa_v[:, l].reshape(dbatch, past, 128).astype(BF16))
        xs, _ = _layer(xs, dbatch, dseq, mod3, lat_idx, layer_weights(l), rope_tabs, ctx_kv, dft_lat, False)

    return (xp.reshape(batch, seq, d), xs.reshape(dbatch, dseq, d), new_diff_k, new_diff_v, new_gqa_k, new_gqa_v)
```

```python
import functools
import math

import numpy as np
import jax
import jax.numpy as jnp
from jax import lax
from jax.experimental import pallas as pl
from jax.experimental.pallas import tpu as pltpu

F32 = jnp.float32
BF16 = jnp.bfloat16

D_MODEL = 1024
GRID_W = 64
ROPE_THETA = 10000.0
DIFF_HEADS = 4
DIFF_HEAD_DIM = 64
GQA_KV_HEADS = 2
GQA_HEAD_DIM = 64
FOURIER_GROUP_DIM = 128
FOURIER_GROUPS = 4
N_BRANCH = 3
BRANCH_WIDTH = 512
C_DQ, C_DK, C_DV, C_GQ, C_GK, C_GV, C_FZ, C_GZ, C_END = 0, 512, 1024, 1536, 2048, 2176, 2304, 2816, 5888
PEER_HEADS = 8
PEER_KEYS = 128
PEER_EXPERTS = PEER_KEYS * PEER_KEYS
PEER_TOPK = 16
DEPTH_FOR_NORM = 4
DEEPNORM_ALPHA = (2 * DEPTH_FOR_NORM) ** 0.25
LN_EPS = 1e-6
ATTN_SCALE = 0.125
SQRT_HALF = math.sqrt(0.5)

LANES = 128
VMEM_LIMIT = 56 * 1024 * 1024
NEG_INF = float("-inf")


def _cparams(sem):
    return pltpu.CompilerParams(dimension_semantics=sem, vmem_limit_bytes=VMEM_LIMIT)


def _ln(x):
    mu = jnp.mean(x, -1, keepdims=True)
    xc = x - mu
    return xc * lax.rsqrt(jnp.mean(xc * xc, -1, keepdims=True) + LN_EPS)


def _dot(a, b):
    return jnp.dot(a, b, preferred_element_type=F32)


def _dot_nt(a, b):
    return lax.dot_general(a, b, (((1,), (1,)), ((), ())), preferred_element_type=F32)


def _mod_kernel(c_ref, w_ref, b_ref, o_ref):
    c = c_ref[...]
    a = c * jax.nn.sigmoid(c)
    o_ref[0] = jnp.dot(a, w_ref[0], preferred_element_type=F32, precision=lax.Precision.HIGHEST) + b_ref[0]


def _mod_call(cs, w_mod, b_mod):
    depth, d, n = w_mod.shape
    rows = cs.shape[0]
    nt = 1536
    return pl.pallas_call(
        _mod_kernel,
        grid=(depth, n // nt),
        in_specs=[pl.BlockSpec((rows, d), lambda l, j: (0, 0)),
                  pl.BlockSpec((1, d, nt), lambda l, j: (l, 0, j)),
                  pl.BlockSpec((1, 1, nt), lambda l, j: (l, 0, j))],
        out_specs=pl.BlockSpec((1, rows, nt), lambda l, j: (l, 0, j)),
        out_shape=jax.ShapeDtypeStruct((depth, rows, n), F32),
        compiler_params=_cparams(("arbitrary", "arbitrary")),
        name="mod_proj",
    )(cs, w_mod, b_mod.reshape(depth, 1, n))


def _rms64(y, bd, g):
    y2 = y * y
    hi = y2.astype(BF16)
    lo = (y2 - hi.astype(F32)).astype(BF16)
    ms = _dot(hi, bd) + _dot(lo, bd)
    return y * lax.rsqrt(ms + LN_EPS) * g


def _rope(y, cos, sin_signed):
    w = y.shape[1]
    reps = w // LANES
    c = jnp.concatenate([cos] * reps, axis=1) if reps > 1 else cos
    s = jnp.concatenate([sin_signed] * reps, axis=1) if reps > 1 else sin_signed
    lane = lax.broadcasted_iota(jnp.int32, y.shape, 1)
    even = (lane & 1) == 0
    partner = jnp.where(even, pltpu.roll(y, w - 1, 1), pltpu.roll(y, 1, 1))
    return y * c + partner * s


def _in_proj_kernel(*refs, rope, emit_kv):
    x_ref, mod_ref, w_ref, qg_ref, kg_ref, bd_ref = refs[:6]
    pos = 6
    if rope:
        cos_ref, sin_ref = refs[6:8]
        pos = 8
    dq_ref, dk_ref, dv_ref, gq_ref, gk_ref, gv_ref, fz_ref, gate_ref = refs[pos:pos + 8]
    pos += 8
    if emit_kv:
        ndk_ref, ndv_ref, ngk_ref, ngv_ref = refs[pos:pos + 4]

    mod = mod_ref[0]
    sh1 = mod[:, 0:D_MODEL]
    sc1 = mod[:, D_MODEL:2 * D_MODEL]
    h = (_ln(x_ref[...]) * (1.0 + sc1) + sh1).astype(BF16)

    def seg(lo, hi):
        return _dot(h, w_ref[:, lo:hi])

    if rope:
        cos = cos_ref[...]
        sin = sin_ref[...]
        rot = lambda y: _rope(y, cos, sin)
    else:
        rot = lambda y: y

    dq_ref[...] = (rot(seg(C_DQ, C_DK)) * ATTN_SCALE).astype(BF16)
    dk = seg(C_DK, C_DV)
    dk_ref[...] = rot(dk).astype(BF16)
    dv = seg(C_DV, C_GQ)
    dv_ref[...] = dv.astype(BF16)
    gq = _rms64(seg(C_GQ, C_GK), bd_ref[...], qg_ref[...])
    gq_ref[...] = (rot(gq) * ATTN_SCALE).astype(BF16)
    gk = _rms64(seg(C_GK, C_GV), bd_ref[0:LANES, 0:LANES], kg_ref[...])
    gk_ref[...] = rot(gk).astype(BF16)
    gv = seg(C_GV, C_FZ)
    gv_ref[...] = gv.astype(BF16)
    fz_ref[...] = seg(C_FZ, C_GZ).astype(BF16)
    for n in range(N_BRANCH):
        lo = C_GZ + n * D_MODEL
        gate_ref[:, n * D_MODEL:(n + 1) * D_MODEL] = jax.nn.sigmoid(seg(lo, lo + D_MODEL))
    if emit_kv:
        ndk_ref[...] = dk
        ndv_ref[...] = dv
        ngk_ref[...] = gk
        ngv_ref[...] = gv


def _in_proj_call(x, mod3, mod_idx, w_in, qg, kg, bd, rope_tabs, seq, tt, emit_kv):
    t, d = x.shape
    rope = rope_tabs is not None
    nblk_seq = seq // tt
    full = lambda shape: pl.BlockSpec(shape, lambda i: (0,) * len(shape))
    tok = lambda w: pl.BlockSpec((tt, w), lambda i: (i, 0))
    in_specs = [tok(d),
                pl.BlockSpec((1, 1, mod3.shape[2]), lambda i: (mod_idx(i), 0, 0)),
                pl.BlockSpec(w_in.shape, lambda i: (0, 0), pipeline_mode=pl.Buffered(1)),
                full(qg.shape), full(kg.shape), full(bd.shape)]
    args = [x, mod3, w_in, qg, kg, bd]
    if rope:
        in_specs += [pl.BlockSpec((tt, LANES), lambda i: (i % nblk_seq, 0))] * 2
        args += list(rope_tabs)
    widths = [512, 512, 512, 512, 128, 128, 512]
    out_specs = [tok(w) for w in widths] + [tok(N_BRANCH * d)]
    out_shape = [jax.ShapeDtypeStruct((t, w), BF16) for w in widths] + [jax.ShapeDtypeStruct((t, N_BRANCH * d), F32)]
    if emit_kv:
        for w in (512, 512, 128, 128):
            out_specs.append(tok(w))
            out_shape.append(jax.ShapeDtypeStruct((t, w), F32))
    return pl.pallas_call(
        functools.partial(_in_proj_kernel, rope=rope, emit_kv=emit_kv),
        grid=(t // tt,),
        in_specs=in_specs,
        out_specs=out_specs,
        out_shape=out_shape,
        compiler_params=_cparams(("arbitrary",)),
        name="in_proj",
    )(*args)


def _attn_kernel(dq_ref, k_ref, v_ref, gq_ref, gk_ref, gv_ref, lam_ref, dng_ref, do_ref, go_ref, *, lam_init):
    tq = dq_ref.shape[1]
    lane = lax.broadcasted_iota(jnp.int32, (tq, LANES), 1)
    is_lo = lane < DIFF_HEAD_DIM
    m_lo = is_lo.astype(F32)
    m_hi = 1.0 - m_lo

    def split_halves(q):
        qf = q.astype(F32)
        return jnp.concatenate([(qf * m_lo).astype(BF16), (qf * m_hi).astype(BF16)], axis=0)

    lv = lam_ref[...]
    lam = (jnp.exp(jnp.sum(lv[0:1] * lv[1:2], keepdims=True))
           - jnp.exp(jnp.sum(lv[2:3] * lv[3:4], keepdims=True)) + lam_init)
    dng = dng_ref[...]

    for h in range(DIFF_HEADS):
        sl = slice(h * LANES, (h + 1) * LANES)
        s = _dot_nt(split_halves(dq_ref[0, :, sl]), k_ref[0, :, sl])
        e = jnp.exp(s - jnp.max(s, -1, keepdims=True))
        a = e / jnp.sum(e, -1, keepdims=True)
        w = (a[:tq] - lam * a[tq:]).astype(BF16)
        o = _dot(w, v_ref[0, :, sl])
        o = o * lax.rsqrt(jnp.mean(o * o, -1, keepdims=True) + LN_EPS) * dng * (1.0 - lam_init)
        do_ref[0, :, sl] = o.astype(BF16)

    for hk in range(GQA_KV_HEADS):
        ksl = slice(hk * LANES, (hk + 1) * LANES)
        for m in range(2):
            c0 = (hk * 2 + m) * LANES
            sl = slice(c0, c0 + LANES)
            s = _dot_nt(split_halves(gq_ref[0, :, sl]), gk_ref[0, :, ksl])
            e = jnp.exp(s - jnp.max(s, -1, keepdims=True))
            l = jnp.sum(e, -1, keepdims=True)
            o = _dot(e.astype(BF16), gv_ref[0, :, ksl]) / l
            go_ref[0, :, sl] = jnp.where(is_lo, o[:tq], o[tq:]).astype(BF16)


def _attn_call(dq, k, v, gq, gk, gv, lamv, dng, lam_init, tq):
    b, s, w = dq.shape
    sk = k.shape[1]
    qspec = pl.BlockSpec((1, tq, w), lambda bi, i: (bi, i, 0))
    kvspec = lambda width: pl.BlockSpec((1, sk, width), lambda bi, i: (bi, 0, 0))
    full = lambda shape: pl.BlockSpec(shape, lambda bi, i: (0,) * len(shape))
    return pl.pallas_call(
        functools.partial(_attn_kernel, lam_init=lam_init),
        grid=(b, s // tq),
        in_specs=[qspec, kvspec(512), kvspec(512), qspec, kvspec(256), kvspec(256),
                  full(lamv.shape), full(dng.shape)],
        out_specs=[qspec, qspec],
        out_shape=[jax.ShapeDtypeStruct((b, s, w), BF16)] * 2,
        compiler_params=_cparams(("arbitrary", "arbitrary")),
        name="attention",
    )(dq, k, v, gq, gk, gv, lamv, dng)


def _fourier_kernel(wc_ref, ws_ref, z_ref, cc_ref, sc_ref, o_ref, *, scale):
    z = z_ref[0]
    p = _dot(wc_ref[...], z).astype(BF16)
    q = _dot(ws_ref[...], z).astype(BF16)
    o_ref[0] = ((_dot(p, cc_ref[...]) - _dot(q, sc_ref[...])) * scale).astype(BF16)


def _fourier_call(z, wc, ws, cc, sc, tm):
    b, s, w = z.shape
    scale = 1.0 / math.sqrt(s * FOURIER_GROUP_DIM)
    return pl.pallas_call(
        functools.partial(_fourier_kernel, scale=scale),
        grid=(s // tm, b),
        in_specs=[pl.BlockSpec((tm, s), lambda i, bi: (i, 0)),
                  pl.BlockSpec((tm, s), lambda i, bi: (i, 0)),
                  pl.BlockSpec((1, s, w), lambda i, bi: (bi, 0, 0)),
                  pl.BlockSpec((w, w), lambda i, bi: (0, 0)),
                  pl.BlockSpec((w, w), lambda i, bi: (0, 0))],
        out_specs=pl.BlockSpec((1, tm, w), lambda i, bi: (bi, i, 0)),
        out_shape=jax.ShapeDtypeStruct((b, s, w), BF16),
        compiler_params=_cparams(("arbitrary", "arbitrary")),
        name="fourier",
    )(wc, ws, z, cc, sc)


def _dft_tables(n):
    k = jnp.arange(n, dtype=jnp.int32)
    m = (k[:, None] * k[None, :]) % n
    ang = m.astype(F32) * (2.0 * math.pi / n)
    return jnp.cos(ang), jnp.sin(ang)


def _merge_kernel(d_ref, g_ref, f_ref, gate_ref, x_ref, mod_ref, wb_ref, wo_ref, lng_ref, lnb_ref, o_ref):
    d = D_MODEL
    m = (gate_ref[:, 0:d] * _dot(d_ref[...], wb_ref[0])
         + gate_ref[:, d:2 * d] * _dot(g_ref[...], wb_ref[1])
         + gate_ref[:, 2 * d:3 * d] * _dot(f_ref[...], wb_ref[2]))
    mix = _dot(m.astype(BF16), wo_ref[...])
    g1 = mod_ref[0][:, 2 * d:3 * d]
    u = DEEPNORM_ALPHA * x_ref[...] + g1 * mix
    o_ref[...] = _ln(u) * lng_ref[...] + lnb_ref[...]


def _merge_call(d_o, g_o, f_o, gates, x, mod3, mod_idx, wb, wo, lng, lnb, tt):
    t, d = x.shape
    full = lambda shape: pl.BlockSpec(shape, lambda i: (0,) * len(shape))
    tok = lambda w: pl.BlockSpec((tt, w), lambda i: (i, 0))
    return pl.pallas_call(
        _merge_kernel,
        grid=(t // tt,),
        in_specs=[tok(512), tok(512), tok(512), tok(N_BRANCH * d), tok(d),
                  pl.BlockSpec((1, 1, mod3.shape[2]), lambda i: (mod_idx(i), 0, 0)),
                  full(wb.shape), full(wo.shape), full(lng.shape), full(lnb.shape)],
        out_specs=tok(d),
        out_shape=jax.ShapeDtypeStruct((t, d), F32),
        compiler_params=_cparams(("arbitrary",)),
        name="merge",
    )(d_o, g_o, f_o, gates, x, mod3, wb, wo, lng, lnb)


N_TOP = PEER_TOPK + 1
PAD_TOP = 24
N_CAND = PAD_TOP + 7 * 8 + (PAD_TOP - 8)


def _top_values(x, n):
    vals = []
    for _ in range(n):
        m = jnp.max(x, axis=0, keepdims=True)
        vals.append(m)
        x = jnp.where(x >= m, NEG_INF, x)
    return vals


def _route_head(s0, s1, av_ref, bv_ref, cand_ref):
    a = _top_values(s0, N_TOP)
    b = _top_values(s1, N_TOP)
    for k in range(N_TOP):
        av_ref[k:k + 1, :] = a[k]
        bv_ref[k:k + 1, :] = b[k]
    av = av_ref[...]
    bv = bv_ref[...]
    cand_ref[0:PAD_TOP, :] = a[0] + bv
    for k in range(1, 8):
        cand_ref[PAD_TOP + 8 * (k - 1):PAD_TOP + 8 * k, :] = a[k] + bv[0:8]
    cand_ref[PAD_TOP + 56:N_CAND, :] = av[8:PAD_TOP] + b[0]
    cand = cand_ref[...]
    c = _top_values(cand, N_TOP)
    tau = 0.5 * (c[PEER_TOPK - 1] + c[PEER_TOPK])
    z = jnp.sum(jnp.where(cand >= tau, jnp.exp(cand - c[0]), 0.0), axis=0, keepdims=True)
    w0 = jnp.exp(s0 - a[0]) / z
    w1 = jnp.exp(s1 - b[0])
    return -s0, s1 - tau, w0, w1


def _route_kernel(x_ref, mod_ref, wq_ref, sk_ref, hb_ref, ns0_ref, a1_ref, w0_ref, w1_ref,
                  av_ref, bv_ref, cand_ref):
    d = D_MODEL
    tt = x_ref.shape[0]
    mod = mod_ref[0]
    sh2 = mod[:, 3 * d:4 * d]
    sc2 = mod[:, 4 * d:5 * d]
    h = _ln(x_ref[...]) * (1.0 + sc2) + sh2
    hb = h.astype(BF16)
    ht = h.T.astype(BF16)
    for cb in range(tt // MM_COLS):
        hb_ref[cb] = ht[:, cb * MM_COLS:(cb + 1) * MM_COLS]
    q = _dot(hb, wq_ref[...])
    lane = lax.broadcasted_iota(jnp.int32, (tt, LANES), 1)
    m_lo = (lane < PEER_KEYS // 2).astype(F32)
    m_hi = 1.0 - m_lo
    av_ref[...] = jnp.full(av_ref.shape, NEG_INF, F32)
    bv_ref[...] = jnp.full(bv_ref.shape, NEG_INF, F32)
    for h in range(PEER_HEADS):
        qh = q[:, h * LANES:(h + 1) * LANES]
        sk = sk_ref[h]
        s0 = _dot_nt(sk, (qh * m_lo).astype(BF16))
        s1 = _dot_nt(sk, (qh * m_hi).astype(BF16))
        ns0, a1, w0, w1 = _route_head(s0, s1, av_ref, bv_ref, cand_ref)
        for tc in range(tt // LANES):
            cols = slice(tc * LANES, (tc + 1) * LANES)
            ns0_ref[h, tc] = ns0[:, cols]
            a1_ref[h, tc] = a1[:, cols]
            w0_ref[h, tc] = w0[:, cols]
            w1_ref[h, tc] = w1[:, cols]


def _route_call(x, mod3, mod_idx, wq, skc, tt):
    t, d = x.shape
    full = lambda shape: pl.BlockSpec(shape, lambda i: (0,) * len(shape))
    rspec = pl.BlockSpec((PEER_HEADS, tt // LANES, PEER_KEYS, LANES), lambda i: (0, i, 0, 0))
    rshape = jax.ShapeDtypeStruct((PEER_HEADS, t // LANES, PEER_KEYS, LANES), F32)
    return pl.pallas_call(
        _route_kernel,
        grid=(t // tt,),
        in_specs=[pl.BlockSpec((tt, d), lambda i: (i, 0)),
                  pl.BlockSpec((1, 1, mod3.shape[2]), lambda i: (mod_idx(i), 0, 0)),
                  full(wq.shape), full(skc.shape)],
        out_specs=[pl.BlockSpec((tt // MM_COLS, d, MM_COLS), lambda i: (i, 0, 0)), rspec, rspec, rspec, rspec],
        out_shape=[jax.ShapeDtypeStruct((t // MM_COLS, d, MM_COLS), BF16), rshape, rshape, rshape, rshape],
        scratch_shapes=[pltpu.VMEM((PAD_TOP, tt), F32), pltpu.VMEM((PAD_TOP, tt), F32),
                        pltpu.VMEM((N_CAND, tt), F32)],
        compiler_params=_cparams(("arbitrary",)),
        name="peer_route",
    )(x, mod3, wq, skc)


MM_COLS = 256
G_ROWS = 64
N_VALUE_PIECES = 2


def _peer_kernel(hb_ref, ns0_ref, a1_ref, w0_ref, w1_ref, u_ref, vt_ref, x_ref, mod_ref, lng_ref, lnb_ref,
                 o_ref, acc_ref, *ga_refs, ni):
    c = pl.program_id(1)
    n_cb = hb_ref.shape[0]
    n_tc = ns0_ref.shape[1]
    per_piece = ni // N_VALUE_PIECES
    kp = per_piece * PEER_KEYS

    @pl.when(c == 0)
    def _():
        acc_ref[...] = jnp.zeros_like(acc_ref)

    i_base = pl.multiple_of(c * ni, ni)

    partial = None
    for ii in range(ni):
        rows = slice(ii * PEER_KEYS, (ii + 1) * PEER_KEYS)
        a_blocks = [_dot(u_ref[rows, :], hb_ref[cb]) for cb in range(n_cb)]
        ga = ga_refs[ii // per_piece]
        r_base = (ii % per_piece) * PEER_KEYS
        for tc in range(n_tc):
            a_tile = a_blocks[tc // 2][:, (tc % 2) * LANES:(tc % 2 + 1) * LANES]
            for jh in range(PEER_KEYS // G_ROWS):
                jrows = slice(jh * G_ROWS, (jh + 1) * G_ROWS)
                g = None
                for h in range(PEER_HEADS):
                    nrow = ns0_ref[h, tc, pl.ds(i_base, ni), :][ii:ii + 1]
                    wrow = w0_ref[h, tc, pl.ds(i_base, ni), :][ii:ii + 1]
                    term = jnp.where(a1_ref[h, tc, jrows, :] >= nrow, w1_ref[h, tc, jrows, :] * wrow, 0.0)
                    g = term if g is None else g + term
                a = a_tile[jrows, :]
                act = 0.5 * a * (1.0 + lax.erf(a * SQRT_HALF))
                ga[r_base + jh * G_ROWS:r_base + (jh + 1) * G_ROWS, tc * LANES:(tc + 1) * LANES] = (
                    (g * act).astype(BF16))
        if (ii + 1) % per_piece == 0:
            p = ii // per_piece
            piece = _dot(vt_ref[:, p * kp:(p + 1) * kp], ga[...])
            partial = piece if partial is None else partial + piece
    acc_ref[...] += partial

    @pl.when(c == pl.num_programs(1) - 1)
    def _():
        d = D_MODEL
        g2 = mod_ref[0][:, 5 * d:6 * d]
        u = DEEPNORM_ALPHA * x_ref[...] + g2 * acc_ref[...].T
        o_ref[...] = _ln(u) * lng_ref[...] + lnb_ref[...]


def _peer_call(hb, ns0, a1, w0, w1, u_tab, vt_tab, x, mod3, mod_idx, lng, lnb, tt, ec):
    t, d = x.shape
    ne = u_tab.shape[0]
    ni = ec // PEER_KEYS
    full = lambda shape: pl.BlockSpec(shape, lambda i, c: (0,) * len(shape))
    rspec = pl.BlockSpec((PEER_HEADS, tt // LANES, PEER_KEYS, LANES), lambda i, c: (0, i, 0, 0))
    return pl.pallas_call(
        functools.partial(_peer_kernel, ni=ni),
        grid=(t // tt, ne // ec),
        in_specs=[pl.BlockSpec((tt // MM_COLS, d, MM_COLS), lambda i, c: (i, 0, 0)), rspec, rspec, rspec, rspec,
                  pl.BlockSpec((ec, d), lambda i, c: (c, 0)),
                  pl.BlockSpec((d, ec), lambda i, c: (0, c)),
                  pl.BlockSpec((tt, d), lambda i, c: (i, 0)),
                  pl.BlockSpec((1, 1, mod3.shape[2]), lambda i, c: (mod_idx(i), 0, 0)),
                  full(lng.shape), full(lnb.shape)],
        out_specs=pl.BlockSpec((tt, d), lambda i, c: (i, 0)),
        out_shape=jax.ShapeDtypeStruct((t, d), F32),
        scratch_shapes=[pltpu.VMEM((d, tt), F32)]
        + [pltpu.VMEM((ec // N_VALUE_PIECES, tt), BF16)] * N_VALUE_PIECES,
        compiler_params=_cparams(("arbitrary", "arbitrary")),
        name="peer_experts",
    )(hb, ns0, a1, w0, w1, u_tab, vt_tab, x, mod3, lng, lnb)


def _pick_tile(n, want):
    t = min(n, want)
    while n % t:
        t //= 2
    return t


def _rope_tables(seq):
    n_rows = seq // GRID_W
    row = jnp.repeat(jnp.arange(n_rows), GRID_W).astype(F32)
    col = jnp.tile(jnp.arange(GRID_W), n_rows).astype(F32)
    n_freq = DIFF_HEAD_DIM // 4
    freqs = ROPE_THETA ** (-jnp.arange(n_freq, dtype=F32) / n_freq)
    ang = jnp.concatenate([row[:, None] * freqs, col[:, None] * freqs], -1)
    cos = jnp.repeat(jnp.cos(ang), 2, axis=-1)
    sin = jnp.repeat(jnp.sin(ang), 2, axis=-1)
    sign = jnp.where(jnp.arange(DIFF_HEAD_DIM) % 2 == 0, -1.0, 1.0).astype(F32)
    reps = LANES // DIFF_HEAD_DIM
    return jnp.tile(cos, (1, reps)), jnp.tile(sin * sign, (1, reps))


def _dup_heads(a):
    b, s, _ = a.shape
    a4 = a.reshape(b, s, GQA_KV_HEADS, 1, GQA_HEAD_DIM)
    return jnp.broadcast_to(a4, (b, s, GQA_KV_HEADS, 2, GQA_HEAD_DIM)).reshape(b, s, 4 * GQA_HEAD_DIM)


def _layer(x, batch, seq, mod3, mod_idx_fn, lw, rope_tabs, ctx_kv, dft, emit_kv):
    t = batch * seq
    tile = lambda want: _pick_tile(seq if rope_tabs is not None else t, want)
    tt = tile(256)
    mod_idx = lambda i: mod_idx_fn(i, max(seq // tt, 1))
    outs = _in_proj_call(x, mod3, mod_idx, lw["w_in"], lw["qg"], lw["kg"], lw["bd"], rope_tabs, seq, tt, emit_kv)
    dq, dk, dv, gq, gk, gv, fz, gates = outs[:8]
    r3 = lambda a: a.reshape(batch, seq, a.shape[-1])
    k_att, v_att, gk_att, gv_att = r3(dk), r3(dv), r3(gk), r3(gv)
    if ctx_kv is not None:
        cdk, cdv, cgk, cgv = ctx_kv
        k_att = jnp.concatenate([k_att, cdk], axis=1)
        v_att = jnp.concatenate([v_att, cdv], axis=1)
        gk_att = jnp.concatenate([gk_att, cgk], axis=1)
        gv_att = jnp.concatenate([gv_att, cgv], axis=1)
    tq = _pick_tile(seq, 128 if k_att.shape[1] > 1024 else 256)
    d_o, g_o = _attn_call(r3(dq), k_att, v_att, r3(gq), _dup_heads(gk_att), _dup_heads(gv_att),
                          lw["diff_lam"], lw["dng"], lw["lam_init"], tq)
    f_o = _fourier_call(r3(fz), dft[0], dft[1], lw["cc"], lw["sc"], _pick_tile(seq, 512))
    flat = lambda a: a.reshape(t, a.shape[-1])
    tm = tile(512)
    mod_idx_m = lambda i: mod_idx_fn(i, max(seq // tm, 1))
    x1 = _merge_call(flat(d_o), flat(g_o), flat(f_o), gates, x, mod3, mod_idx_m, lw["wb"], lw["wo"],
                     lw["lng0"], lw["lnb0"], tm)
    tr = tile(256)
    mod_idx_r = lambda i: mod_idx_fn(i, max(seq // tr, 1))
    hb, ns0, a1, w0, w1 = _route_call(x1, mod3, mod_idx_r, lw["wq"], lw["skc"], tr)
    x2 = _peer_call(hb, ns0, a1, w0, w1, lw["u"], lw["vt"], x1, mod3, mod_idx_m, lw["lng1"], lw["lnb1"], tm, 1024)
    return x2, outs[8:]


def kernel(x_prompt, x_sample, cache_diff_k, cache_diff_v, cache_gqa_k, cache_gqa_v, c, c_ctx, w_mod, b_mod, w_in,
           diff_lam, diff_norm_g, q_norm_g, k_norm_g, w_branch, w_out, ln_g, ln_b, peer_wq, peer_subkeys, peer_u,
           peer_v):
    depth = w_in.shape[0]
    batch, seq, d = x_prompt.shape
    dbatch, dseq, _ = x_sample.shape
    past = cache_diff_k.shape[2]

    rows = ((dbatch + 1 + 7) // 8) * 8
    cs = jnp.zeros((rows, d), F32).at[:dbatch].set(c).at[dbatch].set(c_ctx)
    mod_all = _mod_call(cs, w_mod, b_mod)

    w_in_b = w_in.astype(BF16)
    wb_b = w_branch.astype(BF16)
    wo_b = w_out.astype(BF16)
    wq_b = peer_wq.astype(BF16)
    u_b = peer_u.astype(BF16)
    vt_b = jnp.swapaxes(peer_v, 1, 2).astype(BF16)
    skc = jnp.transpose(peer_subkeys, (0, 1, 3, 2, 4)).reshape(depth, PEER_HEADS, PEER_KEYS, PEER_KEYS).astype(BF16)
    bd = jnp.asarray(np.kron(np.eye(8, dtype=np.float32), np.full((64, 64), 1.0 / 64, np.float32)), BF16)
    ck, sk_ = _dft_tables(FOURIER_GROUP_DIM)
    eye_g = jnp.eye(FOURIER_GROUPS, dtype=F32)
    cc = jnp.kron(eye_g, ck).astype(BF16)
    sc = jnp.kron(eye_g, sk_).astype(BF16)
    dft_ctx = tuple(a.astype(BF16) for a in _dft_tables(seq))
    dft_lat = tuple(a.astype(BF16) for a in _dft_tables(dseq))
    rope_tabs = _rope_tables(dseq)

    def layer_weights(l):
        return {"w_in": w_in_b[l], "qg": jnp.tile(q_norm_g[l], 8)[None, :], "kg": jnp.tile(k_norm_g[l], 2)[None, :],
                "bd": bd, "diff_lam": diff_lam[l], "dng": diff_norm_g[l][None, :],
                "lam_init": 0.8 - 0.6 * math.exp(-0.3 * l), "cc": cc, "sc": sc, "wb": wb_b[l], "wo": wo_b[l],
                "lng0": ln_g[l, 0][None, :], "lnb0": ln_b[l, 0][None, :], "lng1": ln_g[l, 1][None, :],
                "lnb1": ln_b[l, 1][None, :], "wq": wq_b[l], "skc": skc[l], "u": u_b[l], "vt": vt_b[l]}

    xp = x_prompt.reshape(batch * seq, d)
    ctx_idx = lambda i, bps: dbatch
    kv = [[], [], [], []]
    for l in range(depth):
        mod3 = mod_all[l].reshape(rows, 1, 6 * d)
        xp, own = _layer(xp, batch, seq, mod3, ctx_idx, layer_weights(l), None, None, dft_ctx, True)
        for lst, a in zip(kv, own):
            lst.append(a)
    new_diff_k = jnp.stack(kv[0], 0).reshape(depth, batch, seq, DIFF_HEADS, 2 * DIFF_HEAD_DIM).swapaxes(0, 1)
    new_diff_v = jnp.stack(kv[1], 0).reshape(depth, batch, seq, DIFF_HEADS, 2 * DIFF_HEAD_DIM).swapaxes(0, 1)
    new_gqa_k = jnp.stack(kv[2], 0).reshape(depth, batch, seq, GQA_KV_HEADS, GQA_HEAD_DIM).swapaxes(0, 1)
    new_gqa_v = jnp.stack(kv[3], 0).reshape(depth, batch, seq, GQA_KV_HEADS, GQA_HEAD_DIM).swapaxes(0, 1)

    xs = x_sample.reshape(dbatch * dseq, d)
    lat_idx = lambda i, bps: i // bps
    for l in range(depth):
        mod3 = mod_all[l].reshape(rows, 1, 6 * d)
        ctx_kv = (cache_diff_k[:, l].reshape(dbatch, past, 512).astype(BF16),
                  cache_diff_v[:, l].reshape(dbatch, past, 512).astype(BF16),
                  cache_gqa_k[:, l].reshape(dbatch, past, 128).astype(BF16),
                  cache_gqa_v[:, l].reshape(dbatch, past, 128).astype(BF16))
        xs, _ = _layer(xs, dbatch, dseq, mod3, lat_idx, layer_weights(l), rope_tabs, ctx_kv, dft_lat, False)

    return (xp.reshape(batch, seq, d), xs.reshape(dbatch, dseq, d), new_diff_k, new_diff_v, new_gqa_k, new_gqa_v)
```

```python
import functools
import math

import numpy as np
import jax
import jax.numpy as jnp
from jax import lax
from jax.experimental import pallas as pl
from jax.experimental.pallas import tpu as pltpu

F32 = jnp.float32
BF16 = jnp.bfloat16

D_MODEL = 1024
GRID_W = 64
ROPE_THETA = 10000.0
DIFF_HEADS = 4
DIFF_HEAD_DIM = 64
GQA_KV_HEADS = 2
GQA_HEAD_DIM = 64
FOURIER_GROUP_DIM = 128
FOURIER_GROUPS = 4
N_BRANCH = 3
BRANCH_WIDTH = 512
C_DQ, C_DK, C_DV, C_GQ, C_GK, C_GV, C_FZ, C_GZ, C_END = 0, 512, 1024, 1536, 2048, 2176, 2304, 2816, 5888
PEER_HEADS = 8
PEER_KEYS = 128
PEER_EXPERTS = PEER_KEYS * PEER_KEYS
PEER_TOPK = 16
DEPTH_FOR_NORM = 4
DEEPNORM_ALPHA = (2 * DEPTH_FOR_NORM) ** 0.25
LN_EPS = 1e-6
ATTN_SCALE = 0.125 * math.log2(math.e)
SQRT_HALF = math.sqrt(0.5)

LANES = 128
VMEM_LIMIT = 56 * 1024 * 1024
NEG_INF = float("-inf")


def _cparams(sem):
    return pltpu.CompilerParams(dimension_semantics=sem, vmem_limit_bytes=VMEM_LIMIT)


def _ln(x):
    mu = jnp.mean(x, -1, keepdims=True)
    xc = x - mu
    return xc * lax.rsqrt(jnp.mean(xc * xc, -1, keepdims=True) + LN_EPS)


def _dot(a, b):
    return jnp.dot(a, b, preferred_element_type=F32)


def _dot_nt(a, b):
    return lax.dot_general(a, b, (((1,), (1,)), ((), ())), preferred_element_type=F32)


def _mod_kernel(c_ref, w_ref, b_ref, o_ref):
    c = c_ref[...]
    a = c * jax.nn.sigmoid(c)
    o_ref[0] = jnp.dot(a, w_ref[0], preferred_element_type=F32, precision=lax.Precision.HIGHEST) + b_ref[0]


def _mod_call(cs, w_mod, b_mod):
    depth, d, n = w_mod.shape
    rows = cs.shape[0]
    nt = 1536
    return pl.pallas_call(
        _mod_kernel,
        grid=(depth, n // nt),
        in_specs=[pl.BlockSpec((rows, d), lambda l, j: (0, 0)),
                  pl.BlockSpec((1, d, nt), lambda l, j: (l, 0, j)),
                  pl.BlockSpec((1, 1, nt), lambda l, j: (l, 0, j))],
        out_specs=pl.BlockSpec((1, rows, nt), lambda l, j: (l, 0, j)),
        out_shape=jax.ShapeDtypeStruct((depth, rows, n), F32),
        compiler_params=_cparams(("arbitrary", "arbitrary")),
        name="mod_proj",
    )(cs, w_mod, b_mod.reshape(depth, 1, n))


def _rms64(y, bd, g):
    y2 = y * y
    hi = y2.astype(BF16)
    lo = (y2 - hi.astype(F32)).astype(BF16)
    ms = _dot(hi, bd) + _dot(lo, bd)
    return y * lax.rsqrt(ms + LN_EPS) * g


def _rope(y, cos, sin_signed):
    w = y.shape[1]
    reps = w // LANES
    c = jnp.concatenate([cos] * reps, axis=1) if reps > 1 else cos
    s = jnp.concatenate([sin_signed] * reps, axis=1) if reps > 1 else sin_signed
    lane = lax.broadcasted_iota(jnp.int32, y.shape, 1)
    even = (lane & 1) == 0
    partner = jnp.where(even, pltpu.roll(y, w - 1, 1), pltpu.roll(y, 1, 1))
    return y * c + partner * s


def _in_proj_kernel(*refs, rope, emit_kv):
    x_ref, mod_ref, w_ref, qg_ref, kg_ref, bd_ref = refs[:6]
    pos = 6
    if rope:
        cos_ref, sin_ref = refs[6:8]
        pos = 8
    dq_ref, dk_ref, dv_ref, gq_ref, gk_ref, gv_ref, fz_ref, gate_ref = refs[pos:pos + 8]
    pos += 8
    if emit_kv:
        ndk_ref, ndv_ref, ngk_ref, ngv_ref = refs[pos:pos + 4]

    mod = mod_ref[0]
    sh1 = mod[:, 0:D_MODEL]
    sc1 = mod[:, D_MODEL:2 * D_MODEL]
    h = (_ln(x_ref[...]) * (1.0 + sc1) + sh1).astype(BF16)

    def seg(lo, hi):
        return _dot(h, w_ref[:, lo:hi])

    if rope:
        cos = cos_ref[...]
        sin = sin_ref[...]
        rot = lambda y: _rope(y, cos, sin)
    else:
        rot = lambda y: y

    dq_ref[...] = (rot(seg(C_DQ, C_DK)) * ATTN_SCALE).astype(BF16)
    dk = seg(C_DK, C_DV)
    dk_ref[...] = rot(dk).astype(BF16)
    dv = seg(C_DV, C_GQ)
    dv_ref[...] = dv.astype(BF16)
    gq = _rms64(seg(C_GQ, C_GK), bd_ref[...], qg_ref[...])
    gq_ref[...] = (rot(gq) * ATTN_SCALE).astype(BF16)
    gk = _rms64(seg(C_GK, C_GV), bd_ref[0:LANES, 0:LANES], kg_ref[...])
    gk_ref[...] = rot(gk).astype(BF16)
    gv = seg(C_GV, C_FZ)
    gv_ref[...] = gv.astype(BF16)
    fz_ref[...] = seg(C_FZ, C_GZ).astype(BF16)
    for n in range(N_BRANCH):
        lo = C_GZ + n * D_MODEL
        gate_ref[:, n * D_MODEL:(n + 1) * D_MODEL] = jax.nn.sigmoid(seg(lo, lo + D_MODEL))
    if emit_kv:
        ndk_ref[...] = dk
        ndv_ref[...] = dv
        ngk_ref[...] = gk
        ngv_ref[...] = gv


def _in_proj_call(x, mod3, mod_idx, w_in, qg, kg, bd, rope_tabs, seq, tt, emit_kv):
    t, d = x.shape
    rope = rope_tabs is not None
    nblk_seq = seq // tt
    full = lambda shape: pl.BlockSpec(shape, lambda i: (0,) * len(shape))
    tok = lambda w: pl.BlockSpec((tt, w), lambda i: (i, 0))
    in_specs = [tok(d),
                pl.BlockSpec((1, 1, mod3.shape[2]), lambda i: (mod_idx(i), 0, 0)),
                pl.BlockSpec(w_in.shape, lambda i: (0, 0), pipeline_mode=pl.Buffered(1)),
                full(qg.shape), full(kg.shape), full(bd.shape)]
    args = [x, mod3, w_in, qg, kg, bd]
    if rope:
        in_specs += [pl.BlockSpec((tt, LANES), lambda i: (i % nblk_seq, 0))] * 2
        args += list(rope_tabs)
    widths = [512, 512, 512, 512, 128, 128, 512]
    out_specs = [tok(w) for w in widths] + [tok(N_BRANCH * d)]
    out_shape = [jax.ShapeDtypeStruct((t, w), BF16) for w in widths] + [jax.ShapeDtypeStruct((t, N_BRANCH * d), F32)]
    if emit_kv:
        for w in (512, 512, 128, 128):
            out_specs.append(tok(w))
            out_shape.append(jax.ShapeDtypeStruct((t, w), F32))
    return pl.pallas_call(
        functools.partial(_in_proj_kernel, rope=rope, emit_kv=emit_kv),
        grid=(t // tt,),
        in_specs=in_specs,
        out_specs=out_specs,
        out_shape=out_shape,
        compiler_params=_cparams(("arbitrary",)),
        name="in_proj",
    )(*args)


def _attn_kernel(dq_ref, k_ref, v_ref, gq_ref, gk_ref, gv_ref, lam_ref, dng_ref, do_ref, go_ref, *, lam_init):
    tq = dq_ref.shape[1]
    lane = lax.broadcasted_iota(jnp.int32, (tq, LANES), 1)
    is_lo = lane < DIFF_HEAD_DIM
    m_lo = is_lo.astype(F32)
    m_hi = 1.0 - m_lo

    def split_halves(q):
        qf = q.astype(F32)
        return jnp.concatenate([(qf * m_lo).astype(BF16), (qf * m_hi).astype(BF16)], axis=0)

    lv = lam_ref[...]
    lam = (jnp.exp(jnp.sum(lv[0:1] * lv[1:2], keepdims=True))
           - jnp.exp(jnp.sum(lv[2:3] * lv[3:4], keepdims=True)) + lam_init)
    dng = dng_ref[...]

    for h in range(DIFF_HEADS):
        sl = slice(h * LANES, (h + 1) * LANES)
        s = _dot_nt(split_halves(dq_ref[0, :, sl]), k_ref[0, :, sl])
        e = jnp.exp2(s - jnp.max(s, -1, keepdims=True))
        a = e / jnp.sum(e, -1, keepdims=True)
        w = (a[:tq] - lam * a[tq:]).astype(BF16)
        o = _dot(w, v_ref[0, :, sl])
        o = o * lax.rsqrt(jnp.mean(o * o, -1, keepdims=True) + LN_EPS) * dng * (1.0 - lam_init)
        do_ref[0, :, sl] = o.astype(BF16)

    for hk in range(GQA_KV_HEADS):
        ksl = slice(hk * LANES, (hk + 1) * LANES)
        for m in range(2):
            c0 = (hk * 2 + m) * LANES
            sl = slice(c0, c0 + LANES)
            s = _dot_nt(split_halves(gq_ref[0, :, sl]), gk_ref[0, :, ksl])
            e = jnp.exp2(s - jnp.max(s, -1, keepdims=True))
            l = jnp.sum(e, -1, keepdims=True)
            o = _dot(e.astype(BF16), gv_ref[0, :, ksl]) / l
            go_ref[0, :, sl] = jnp.where(is_lo, o[:tq], o[tq:]).astype(BF16)


def _attn_call(dq, k, v, gq, gk, gv, lamv, dng, lam_init, tq):
    b, s, w = dq.shape
    sk = k.shape[1]
    qspec = pl.BlockSpec((1, tq, w), lambda bi, i: (bi, i, 0))
    kvspec = lambda width: pl.BlockSpec((1, sk, width), lambda bi, i: (bi, 0, 0))
    full = lambda shape: pl.BlockSpec(shape, lambda bi, i: (0,) * len(shape))
    return pl.pallas_call(
        functools.partial(_attn_kernel, lam_init=lam_init),
        grid=(b, s // tq),
        in_specs=[qspec, kvspec(512), kvspec(512), qspec, kvspec(256), kvspec(256),
                  full(lamv.shape), full(dng.shape)],
        out_specs=[qspec, qspec],
        out_shape=[jax.ShapeDtypeStruct((b, s, w), BF16)] * 2,
        compiler_params=_cparams(("arbitrary", "arbitrary")),
        name="attention",
    )(dq, k, v, gq, gk, gv, lamv, dng)


def _fourier_kernel(wc_ref, ws_ref, z_ref, cc_ref, sc_ref, o_ref, *, scale):
    z = z_ref[0]
    p = _dot(wc_ref[...], z).astype(BF16)
    q = _dot(ws_ref[...], z).astype(BF16)
    o_ref[0] = ((_dot(p, cc_ref[...]) - _dot(q, sc_ref[...])) * scale).astype(BF16)


def _fourier_call(z, wc, ws, cc, sc, tm):
    b, s, w = z.shape
    scale = 1.0 / math.sqrt(s * FOURIER_GROUP_DIM)
    return pl.pallas_call(
        functools.partial(_fourier_kernel, scale=scale),
        grid=(s // tm, b),
        in_specs=[pl.BlockSpec((tm, s), lambda i, bi: (i, 0)),
                  pl.BlockSpec((tm, s), lambda i, bi: (i, 0)),
                  pl.BlockSpec((1, s, w), lambda i, bi: (bi, 0, 0)),
                  pl.BlockSpec((w, w), lambda i, bi: (0, 0)),
                  pl.BlockSpec((w, w), lambda i, bi: (0, 0))],
        out_specs=pl.BlockSpec((1, tm, w), lambda i, bi: (bi, i, 0)),
        out_shape=jax.ShapeDtypeStruct((b, s, w), BF16),
        compiler_params=_cparams(("arbitrary", "arbitrary")),
        name="fourier",
    )(wc, ws, z, cc, sc)


def _dft_tables(n):
    k = jnp.arange(n, dtype=jnp.int32)
    m = (k[:, None] * k[None, :]) % n
    ang = m.astype(F32) * (2.0 * math.pi / n)
    return jnp.cos(ang), jnp.sin(ang)


def _merge_kernel(d_ref, g_ref, f_ref, gate_ref, x_ref, mod_ref, wb_ref, wo_ref, lng_ref, lnb_ref, o_ref):
    d = D_MODEL
    m = (gate_ref[:, 0:d] * _dot(d_ref[...], wb_ref[0])
         + gate_ref[:, d:2 * d] * _dot(g_ref[...], wb_ref[1])
         + gate_ref[:, 2 * d:3 * d] * _dot(f_ref[...], wb_ref[2]))
    mix = _dot(m.astype(BF16), wo_ref[...])
    g1 = mod_ref[0][:, 2 * d:3 * d]
    u = DEEPNORM_ALPHA * x_ref[...] + g1 * mix
    o_ref[...] = _ln(u) * lng_ref[...] + lnb_ref[...]


def _merge_call(d_o, g_o, f_o, gates, x, mod3, mod_idx, wb, wo, lng, lnb, tt):
    t, d = x.shape
    full = lambda shape: pl.BlockSpec(shape, lambda i: (0,) * len(shape))
    tok = lambda w: pl.BlockSpec((tt, w), lambda i: (i, 0))
    return pl.pallas_call(
        _merge_kernel,
        grid=(t // tt,),
        in_specs=[tok(512), tok(512), tok(512), tok(N_BRANCH * d), tok(d),
                  pl.BlockSpec((1, 1, mod3.shape[2]), lambda i: (mod_idx(i), 0, 0)),
                  full(wb.shape), full(wo.shape), full(lng.shape), full(lnb.shape)],
        out_specs=tok(d),
        out_shape=jax.ShapeDtypeStruct((t, d), F32),
        compiler_params=_cparams(("arbitrary",)),
        name="merge",
    )(d_o, g_o, f_o, gates, x, mod3, wb, wo, lng, lnb)


N_TOP = PEER_TOPK + 1
PAD_TOP = 24
N_CAND = PAD_TOP + 7 * 8 + (PAD_TOP - 8)


def _top_values(x, n):
    vals = []
    for _ in range(n):
        m = jnp.max(x, axis=0, keepdims=True)
        vals.append(m)
        x = jnp.where(x >= m, NEG_INF, x)
    return vals


def _route_head(s0, s1, av_ref, bv_ref, cand_ref):
    a = _top_values(s0, N_TOP)
    b = _top_values(s1, N_TOP)
    for k in range(N_TOP):
        av_ref[k:k + 1, :] = a[k]
        bv_ref[k:k + 1, :] = b[k]
    av = av_ref[...]
    bv = bv_ref[...]
    cand_ref[0:PAD_TOP, :] = a[0] + bv
    for k in range(1, 8):
        cand_ref[PAD_TOP + 8 * (k - 1):PAD_TOP + 8 * k, :] = a[k] + bv[0:8]
    cand_ref[PAD_TOP + 56:N_CAND, :] = av[8:PAD_TOP] + b[0]
    cand = cand_ref[...]
    c = _top_values(cand, N_TOP)
    tau = 0.5 * (c[PEER_TOPK - 1] + c[PEER_TOPK])
    z = jnp.sum(jnp.where(cand >= tau, jnp.exp(cand - c[0]), 0.0), axis=0, keepdims=True)
    w0 = jnp.exp(s0 - a[0]) / z
    w1 = jnp.exp(s1 - b[0])
    return -s0, s1 - tau, w0, w1


def _route_kernel(x_ref, mod_ref, wq_ref, sk_ref, hb_ref, ns0_ref, a1_ref, w0_ref, w1_ref,
                  av_ref, bv_ref, cand_ref):
    d = D_MODEL
    tt = x_ref.shape[0]
    mod = mod_ref[0]
    sh2 = mod[:, 3 * d:4 * d]
    sc2 = mod[:, 4 * d:5 * d]
    h = _ln(x_ref[...]) * (1.0 + sc2) + sh2
    hb = h.astype(BF16)
    ht = h.T.astype(BF16)
    for cb in range(tt // MM_COLS):
        hb_ref[cb] = ht[:, cb * MM_COLS:(cb + 1) * MM_COLS]
    q = _dot(hb, wq_ref[...])
    lane = lax.broadcasted_iota(jnp.int32, (tt, LANES), 1)
    m_lo = (lane < PEER_KEYS // 2).astype(F32)
    m_hi = 1.0 - m_lo
    av_ref[...] = jnp.full(av_ref.shape, NEG_INF, F32)
    bv_ref[...] = jnp.full(bv_ref.shape, NEG_INF, F32)
    for h in range(PEER_HEADS):
        qh = q[:, h * LANES:(h + 1) * LANES]
        sk = sk_ref[h]
        s0 = _dot_nt(sk, (qh * m_lo).astype(BF16))
        s1 = _dot_nt(sk, (qh * m_hi).astype(BF16))
        ns0, a1, w0, w1 = _route_head(s0, s1, av_ref, bv_ref, cand_ref)
        for tc in range(tt // LANES):
            cols = slice(tc * LANES, (tc + 1) * LANES)
            ns0_ref[h, tc] = ns0[:, cols]
            a1_ref[h, tc] = a1[:, cols]
            w0_ref[h, tc] = w0[:, cols]
            w1_ref[h, tc] = w1[:, cols]


def _route_call(x, mod3, mod_idx, wq, skc, tt):
    t, d = x.shape
    full = lambda shape: pl.BlockSpec(shape, lambda i: (0,) * len(shape))
    rspec = pl.BlockSpec((PEER_HEADS, tt // LANES, PEER_KEYS, LANES), lambda i: (0, i, 0, 0))
    rshape = jax.ShapeDtypeStruct((PEER_HEADS, t // LANES, PEER_KEYS, LANES), F32)
    return pl.pallas_call(
        _route_kernel,
        grid=(t // tt,),
        in_specs=[pl.BlockSpec((tt, d), lambda i: (i, 0)),
                  pl.BlockSpec((1, 1, mod3.shape[2]), lambda i: (mod_idx(i), 0, 0)),
                  full(wq.shape), full(skc.shape)],
        out_specs=[pl.BlockSpec((tt // MM_COLS, d, MM_COLS), lambda i: (i, 0, 0)), rspec, rspec, rspec, rspec],
        out_shape=[jax.ShapeDtypeStruct((t // MM_COLS, d, MM_COLS), BF16), rshape, rshape, rshape, rshape],
        scratch_shapes=[pltpu.VMEM((PAD_TOP, tt), F32), pltpu.VMEM((PAD_TOP, tt), F32),
                        pltpu.VMEM((N_CAND, tt), F32)],
        compiler_params=_cparams(("arbitrary",)),
        name="peer_route",
    )(x, mod3, wq, skc)


MM_COLS = 256
G_ROWS = 64


def _peer_kernel(hb_ref, ns0_ref, a1_ref, w0_ref, w1_ref, u_ref, vt_ref, x_ref, mod_ref, lng_ref, lnb_ref,
                 o_ref, acc_ref, a_ref, ga_ref, *, ni):
    c = pl.program_id(1)
    n_cb = hb_ref.shape[0]
    n_tc = ns0_ref.shape[1]

    @pl.when(c == 0)
    def _():
        acc_ref[...] = jnp.zeros_like(acc_ref)

    for cb in range(n_cb):
        a_ref[:, cb * MM_COLS:(cb + 1) * MM_COLS] = _dot(u_ref[...], hb_ref[cb])

    i_base = pl.multiple_of(c * ni, ni)
    for tc in range(n_tc):
        cols = slice(tc * LANES, (tc + 1) * LANES)
        nrows = [ns0_ref[h, tc, pl.ds(i_base, ni), :] for h in range(PEER_HEADS)]
        wrows = [w0_ref[h, tc, pl.ds(i_base, ni), :] for h in range(PEER_HEADS)]
        for ii in range(ni):
            for jh in range(PEER_KEYS // G_ROWS):
                jrows = slice(jh * G_ROWS, (jh + 1) * G_ROWS)
                rows = slice(ii * PEER_KEYS + jh * G_ROWS, ii * PEER_KEYS + (jh + 1) * G_ROWS)
                g = None
                for h in range(PEER_HEADS):
                    term = jnp.where(a1_ref[h, tc, jrows, :] >= nrows[h][ii:ii + 1],
                                     w1_ref[h, tc, jrows, :] * wrows[h][ii:ii + 1], 0.0)
                    g = term if g is None else g + term
                a = a_ref[rows, cols]
                act = 0.5 * a * (1.0 + lax.erf(a * SQRT_HALF))
                ga_ref[rows, cols] = (g * act).astype(BF16)

    acc_ref[...] += _dot(vt_ref[...], ga_ref[...])

    @pl.when(c == pl.num_programs(1) - 1)
    def _():
        d = D_MODEL
        g2 = mod_ref[0][:, 5 * d:6 * d]
        u = DEEPNORM_ALPHA * x_ref[...] + g2 * acc_ref[...].T
        o_ref[...] = _ln(u) * lng_ref[...] + lnb_ref[...]


def _peer_call(hb, ns0, a1, w0, w1, u_tab, vt_tab, x, mod3, mod_idx, lng, lnb, tt, ec):
    t, d = x.shape
    ne = u_tab.shape[0]
    ni = ec // PEER_KEYS
    full = lambda shape: pl.BlockSpec(shape, lambda i, c: (0,) * len(shape))
    rspec = pl.BlockSpec((PEER_HEADS, tt // LANES, PEER_KEYS, LANES), lambda i, c: (0, i, 0, 0))
    return pl.pallas_call(
        functools.partial(_peer_kernel, ni=ni),
        grid=(t // tt, ne // ec),
        in_specs=[pl.BlockSpec((tt // MM_COLS, d, MM_COLS), lambda i, c: (i, 0, 0)), rspec, rspec, rspec, rspec,
                  pl.BlockSpec((ec, d), lambda i, c: (c, 0)),
                  pl.BlockSpec((d, ec), lambda i, c: (0, c)),
                  pl.BlockSpec((tt, d), lambda i, c: (i, 0)),
                  pl.BlockSpec((1, 1, mod3.shape[2]), lambda i, c: (mod_idx(i), 0, 0)),
                  full(lng.shape), full(lnb.shape)],
        out_specs=pl.BlockSpec((tt, d), lambda i, c: (i, 0)),
        out_shape=jax.ShapeDtypeStruct((t, d), F32),
        scratch_shapes=[pltpu.VMEM((d, tt), F32), pltpu.VMEM((ec, tt), F32), pltpu.VMEM((ec, tt), BF16)],
        compiler_params=_cparams(("arbitrary", "arbitrary")),
        name="peer_experts",
    )(hb, ns0, a1, w0, w1, u_tab, vt_tab, x, mod3, lng, lnb)


def _pick_tile(n, want):
    t = min(n, want)
    while n % t:
        t //= 2
    return t


def _rope_tables(seq):
    n_rows = seq // GRID_W
    row = jnp.repeat(jnp.arange(n_rows), GRID_W).astype(F32)
    col = jnp.tile(jnp.arange(GRID_W), n_rows).astype(F32)
    n_freq = DIFF_HEAD_DIM // 4
    freqs = ROPE_THETA ** (-jnp.arange(n_freq, dtype=F32) / n_freq)
    ang = jnp.concatenate([row[:, None] * freqs, col[:, None] * freqs], -1)
    cos = jnp.repeat(jnp.cos(ang), 2, axis=-1)
    sin = jnp.repeat(jnp.sin(ang), 2, axis=-1)
    sign = jnp.where(jnp.arange(DIFF_HEAD_DIM) % 2 == 0, -1.0, 1.0).astype(F32)
    reps = LANES // DIFF_HEAD_DIM
    return jnp.tile(cos, (1, reps)), jnp.tile(sin * sign, (1, reps))


def _dup_heads(a):
    b, s, _ = a.shape
    a4 = a.reshape(b, s, GQA_KV_HEADS, 1, GQA_HEAD_DIM)
    return jnp.broadcast_to(a4, (b, s, GQA_KV_HEADS, 2, GQA_HEAD_DIM)).reshape(b, s, 4 * GQA_HEAD_DIM)


def _layer(x, batch, seq, mod3, mod_idx_fn, lw, rope_tabs, ctx_kv, dft, emit_kv):
    t = batch * seq
    tile = lambda want: _pick_tile(seq if rope_tabs is not None else t, want)
    tt = tile(256)
    mod_idx = lambda i: mod_idx_fn(i, max(seq // tt, 1))
    outs = _in_proj_call(x, mod3, mod_idx, lw["w_in"], lw["qg"], lw["kg"], lw["bd"], rope_tabs, seq, tt, emit_kv)
    dq, dk, dv, gq, gk, gv, fz, gates = outs[:8]
    r3 = lambda a: a.reshape(batch, seq, a.shape[-1])
    k_att, v_att, gk_att, gv_att = r3(dk), r3(dv), r3(gk), r3(gv)
    if ctx_kv is not None:
        cdk, cdv, cgk, cgv = ctx_kv
        k_att = jnp.concatenate([k_att, cdk], axis=1)
        v_att = jnp.concatenate([v_att, cdv], axis=1)
        gk_att = jnp.concatenate([gk_att, cgk], axis=1)
        gv_att = jnp.concatenate([gv_att, cgv], axis=1)
    tq = _pick_tile(seq, 128 if k_att.shape[1] > 1024 else 256)
    d_o, g_o = _attn_call(r3(dq), k_att, v_att, r3(gq), _dup_heads(gk_att), _dup_heads(gv_att),
                          lw["diff_lam"], lw["dng"], lw["lam_init"], tq)
    f_o = _fourier_call(r3(fz), dft[0], dft[1], lw["cc"], lw["sc"], _pick_tile(seq, 512))
    flat = lambda a: a.reshape(t, a.shape[-1])
    tm = tile(512)
    mod_idx_m = lambda i: mod_idx_fn(i, max(seq // tm, 1))
    x1 = _merge_call(flat(d_o), flat(g_o), flat(f_o), gates, x, mod3, mod_idx_m, lw["wb"], lw["wo"],
                     lw["lng0"], lw["lnb0"], tm)
    tr = tile(256)
    mod_idx_r = lambda i: mod_idx_fn(i, max(seq // tr, 1))
    hb, ns0, a1, w0, w1 = _route_call(x1, mod3, mod_idx_r, lw["wq"], lw["skc"], tr)
    x2 = _peer_call(hb, ns0, a1, w0, w1, lw["u"], lw["vt"], x1, mod3, mod_idx_m, lw["lng1"], lw["lnb1"], tm, 1024)
    return x2, outs[8:]


def kernel(x_prompt, x_sample, cache_diff_k, cache_diff_v, cache_gqa_k, cache_gqa_v, c, c_ctx, w_mod, b_mod, w_in,
           diff_lam, diff_norm_g, q_norm_g, k_norm_g, w_branch, w_out, ln_g, ln_b, peer_wq, peer_subkeys, peer_u,
           peer_v):
    depth = w_in.shape[0]
    batch, seq, d = x_prompt.shape
    dbatch, dseq, _ = x_sample.shape
    past = cache_diff_k.shape[2]

    rows = ((dbatch + 1 + 7) // 8) * 8
    cs = jnp.zeros((rows, d), F32).at[:dbatch].set(c).at[dbatch].set(c_ctx)
    mod_all = _mod_call(cs, w_mod, b_mod)

    w_in_b = w_in.astype(BF16)
    wb_b = w_branch.astype(BF16)
    wo_b = w_out.astype(BF16)
    wq_b = peer_wq.astype(BF16)
    u_b = peer_u.astype(BF16)
    vt_b = jnp.swapaxes(peer_v, 1, 2).astype(BF16)
    skc = jnp.transpose(peer_subkeys, (0, 1, 3, 2, 4)).reshape(depth, PEER_HEADS, PEER_KEYS, PEER_KEYS).astype(BF16)
    bd = jnp.asarray(np.kron(np.eye(8, dtype=np.float32), np.full((64, 64), 1.0 / 64, np.float32)), BF16)
    ck, sk_ = _dft_tables(FOURIER_GROUP_DIM)
    eye_g = jnp.eye(FOURIER_GROUPS, dtype=F32)
    cc = jnp.kron(eye_g, ck).astype(BF16)
    sc = jnp.kron(eye_g, sk_).astype(BF16)
    dft_ctx = tuple(a.astype(BF16) for a in _dft_tables(seq))
    dft_lat = tuple(a.astype(BF16) for a in _dft_tables(dseq))
    rope_tabs = _rope_tables(dseq)

    def layer_weights(l):
        return {"w_in": w_in_b[l], "qg": jnp.tile(q_norm_g[l], 8)[None, :], "kg": jnp.tile(k_norm_g[l], 2)[None, :],
                "bd": bd, "diff_lam": diff_lam[l], "dng": diff_norm_g[l][None, :],
                "lam_init": 0.8 - 0.6 * math.exp(-0.3 * l), "cc": cc, "sc": sc, "wb": wb_b[l], "wo": wo_b[l],
                "lng0": ln_g[l, 0][None, :], "lnb0": ln_b[l, 0][None, :], "lng1": ln_g[l, 1][None, :],
                "lnb1": ln_b[l, 1][None, :], "wq": wq_b[l], "skc": skc[l], "u": u_b[l], "vt": vt_b[l]}

    xp = x_prompt.reshape(batch * seq, d)
    ctx_idx = lambda i, bps: dbatch
    kv = [[], [], [], []]
    for l in range(depth):
        mod3 = mod_all[l].reshape(rows, 1, 6 * d)
        xp, own = _layer(xp, batch, seq, mod3, ctx_idx, layer_weights(l), None, None, dft_ctx, True)
        for lst, a in zip(kv, own):
            lst.append(a)
    new_diff_k = jnp.stack(kv[0], 0).reshape(depth, batch, seq, DIFF_HEADS, 2 * DIFF_HEAD_DIM).swapaxes(0, 1)
    new_diff_v = jnp.stack(kv[1], 0).reshape(depth, batch, seq, DIFF_HEADS, 2 * DIFF_HEAD_DIM).swapaxes(0, 1)
    new_gqa_k = jnp.stack(kv[2], 0).reshape(depth, batch, seq, GQA_KV_HEADS, GQA_HEAD_DIM).swapaxes(0, 1)
    new_gqa_v = jnp.stack(kv[3], 0).reshape(depth, batch, seq, GQA_KV_HEADS, GQA_HEAD_DIM).swapaxes(0, 1)

    xs = x_sample.reshape(dbatch * dseq, d)
    lat_idx = lambda i, bps: i // bps
    for l in range(depth):
        mod3 = mod_all[l].reshape(rows, 1, 6 * d)
        ctx_kv = (cache_diff_k[:, l].reshape(dbatch, past, 512).astype(BF16),
                  cache_diff_v[:, l].reshape(dbatch, past, 512).astype(BF16),
                  cache_gqa_k[:, l].reshape(dbatch, past, 128).astype(BF16),
                  cache_gqa_v[:, l].reshape(dbatch, past, 128).astype(BF16))
        xs, _ = _layer(xs, dbatch, dseq, mod3, lat_idx, layer_weights(l), rope_tabs, ctx_kv, dft_lat, False)

    return (xp.reshape(batch, seq, d), xs.reshape(dbatch, dseq, d), new_diff_k, new_diff_v, new_gqa_k, new_gqa_v)
```

```python
import functools
import math

import numpy as np
import jax
import jax.numpy as jnp
from jax import lax
from jax.experimental import pallas as pl
from jax.experimental.pallas import tpu as pltpu

F32 = jnp.float32
BF16 = jnp.bfloat16

D_MODEL = 1024
GRID_W = 64
ROPE_THETA = 10000.0
DIFF_HEADS = 4
DIFF_HEAD_DIM = 64
GQA_KV_HEADS = 2
GQA_HEAD_DIM = 64
FOURIER_GROUP_DIM = 128
FOURIER_GROUPS = 4
N_BRANCH = 3
BRANCH_WIDTH = 512
C_DQ, C_DK, C_DV, C_GQ, C_GK, C_GV, C_FZ, C_GZ, C_END = 0, 512, 1024, 1536, 2048, 2176, 2304, 2816, 5888
PEER_HEADS = 8
PEER_KEYS = 128
PEER_EXPERTS = PEER_KEYS * PEER_KEYS
PEER_TOPK = 16
DEPTH_FOR_NORM = 4
DEEPNORM_ALPHA = (2 * DEPTH_FOR_NORM) ** 0.25
LN_EPS = 1e-6
ATTN_SCALE = 0.125 * math.log2(math.e)
SQRT_HALF = math.sqrt(0.5)

LANES = 128
VMEM_LIMIT = 56 * 1024 * 1024
NEG_INF = float("-inf")


def _cparams(sem):
    return pltpu.CompilerParams(dimension_semantics=sem, vmem_limit_bytes=VMEM_LIMIT)


def _ln(x):
    mu = jnp.mean(x, -1, keepdims=True)
    xc = x - mu
    return xc * lax.rsqrt(jnp.mean(xc * xc, -1, keepdims=True) + LN_EPS)


def _dot(a, b):
    return jnp.dot(a, b, preferred_element_type=F32)


def _dot_nt(a, b):
    return lax.dot_general(a, b, (((1,), (1,)), ((), ())), preferred_element_type=F32)


def _mod_kernel(c_ref, w_ref, b_ref, o_ref):
    c = c_ref[...]
    a = c * jax.nn.sigmoid(c)
    o_ref[0] = jnp.dot(a, w_ref[0], preferred_element_type=F32, precision=lax.Precision.HIGHEST) + b_ref[0]


def _mod_call(cs, w_mod, b_mod):
    depth, d, n = w_mod.shape
    rows = cs.shape[0]
    nt = 1536
    return pl.pallas_call(
        _mod_kernel,
        grid=(depth, n // nt),
        in_specs=[pl.BlockSpec((rows, d), lambda l, j: (0, 0)),
                  pl.BlockSpec((1, d, nt), lambda l, j: (l, 0, j)),
                  pl.BlockSpec((1, 1, nt), lambda l, j: (l, 0, j))],
        out_specs=pl.BlockSpec((1, rows, nt), lambda l, j: (l, 0, j)),
        out_shape=jax.ShapeDtypeStruct((depth, rows, n), F32),
        compiler_params=_cparams(("arbitrary", "arbitrary")),
        name="mod_proj",
    )(cs, w_mod, b_mod.reshape(depth, 1, n))


def _rms64(y, bd, g):
    y2 = y * y
    hi = y2.astype(BF16)
    lo = (y2 - hi.astype(F32)).astype(BF16)
    ms = _dot(hi, bd) + _dot(lo, bd)
    return y * lax.rsqrt(ms + LN_EPS) * g


def _rope(y, cos, sin_signed):
    w = y.shape[1]
    reps = w // LANES
    c = jnp.concatenate([cos] * reps, axis=1) if reps > 1 else cos
    s = jnp.concatenate([sin_signed] * reps, axis=1) if reps > 1 else sin_signed
    lane = lax.broadcasted_iota(jnp.int32, y.shape, 1)
    even = (lane & 1) == 0
    partner = jnp.where(even, pltpu.roll(y, w - 1, 1), pltpu.roll(y, 1, 1))
    return y * c + partner * s


def _in_proj_kernel(*refs, rope, emit_kv):
    x_ref, mod_ref, w_ref, qg_ref, kg_ref, bd_ref = refs[:6]
    pos = 6
    if rope:
        cos_ref, sin_ref = refs[6:8]
        pos = 8
    dq_ref, dk_ref, dv_ref, gq_ref, gk_ref, gv_ref, fz_ref, gate_ref = refs[pos:pos + 8]
    pos += 8
    if emit_kv:
        ndk_ref, ndv_ref, ngk_ref, ngv_ref = refs[pos:pos + 4]

    mod = mod_ref[0]
    sh1 = mod[:, 0:D_MODEL]
    sc1 = mod[:, D_MODEL:2 * D_MODEL]
    h = (_ln(x_ref[...]) * (1.0 + sc1) + sh1).astype(BF16)

    def seg(lo, hi):
        return _dot(h, w_ref[:, lo:hi])

    if rope:
        cos = cos_ref[...]
        sin = sin_ref[...]
        rot = lambda y: _rope(y, cos, sin)
    else:
        rot = lambda y: y

    dq_ref[...] = (rot(seg(C_DQ, C_DK)) * ATTN_SCALE).astype(BF16)
    dk = seg(C_DK, C_DV)
    dk_ref[...] = rot(dk).astype(BF16)
    dv = seg(C_DV, C_GQ)
    dv_ref[...] = dv.astype(BF16)
    gq = _rms64(seg(C_GQ, C_GK), bd_ref[...], qg_ref[...])
    gq_ref[...] = (rot(gq) * ATTN_SCALE).astype(BF16)
    gk = _rms64(seg(C_GK, C_GV), bd_ref[0:LANES, 0:LANES], kg_ref[...])
    gk_ref[...] = rot(gk).astype(BF16)
    gv = seg(C_GV, C_FZ)
    gv_ref[...] = gv.astype(BF16)
    fz_ref[...] = seg(C_FZ, C_GZ).astype(BF16)
    for n in range(N_BRANCH):
        lo = C_GZ + n * D_MODEL
        gate_ref[:, n * D_MODEL:(n + 1) * D_MODEL] = jax.nn.sigmoid(seg(lo, lo + D_MODEL))
    if emit_kv:
        ndk_ref[...] = dk
        ndv_ref[...] = dv
        ngk_ref[...] = gk
        ngv_ref[...] = gv


def _in_proj_call(x, mod3, mod_idx, w_in, qg, kg, bd, rope_tabs, seq, tt, emit_kv):
    t, d = x.shape
    rope = rope_tabs is not None
    nblk_seq = seq // tt
    full = lambda shape: pl.BlockSpec(shape, lambda i: (0,) * len(shape))
    tok = lambda w: pl.BlockSpec((tt, w), lambda i: (i, 0))
    in_specs = [tok(d),
                pl.BlockSpec((1, 1, mod3.shape[2]), lambda i: (mod_idx(i), 0, 0)),
                pl.BlockSpec(w_in.shape, lambda i: (0, 0), pipeline_mode=pl.Buffered(1)),
                full(qg.shape), full(kg.shape), full(bd.shape)]
    args = [x, mod3, w_in, qg, kg, bd]
    if rope:
        in_specs += [pl.BlockSpec((tt, LANES), lambda i: (i % nblk_seq, 0))] * 2
        args += list(rope_tabs)
    widths = [512, 512, 512, 512, 128, 128, 512]
    out_specs = [tok(w) for w in widths] + [tok(N_BRANCH * d)]
    out_shape = [jax.ShapeDtypeStruct((t, w), BF16) for w in widths] + [jax.ShapeDtypeStruct((t, N_BRANCH * d), F32)]
    if emit_kv:
        for w in (512, 512, 128, 128):
            out_specs.append(tok(w))
            out_shape.append(jax.ShapeDtypeStruct((t, w), F32))
    return pl.pallas_call(
        functools.partial(_in_proj_kernel, rope=rope, emit_kv=emit_kv),
        grid=(t // tt,),
        in_specs=in_specs,
        out_specs=out_specs,
        out_shape=out_shape,
        compiler_params=_cparams(("arbitrary",)),
        name="in_proj",
    )(*args)


def _attn_kernel(dq_ref, k_ref, v_ref, gq_ref, gk_ref, gv_ref, lam_ref, dng_ref, do_ref, go_ref, *, lam_init):
    tq = dq_ref.shape[1]
    lane = lax.broadcasted_iota(jnp.int32, (tq, LANES), 1)
    is_lo = lane < DIFF_HEAD_DIM
    m_lo = is_lo.astype(F32)
    m_hi = 1.0 - m_lo

    def split_halves(q):
        qf = q.astype(F32)
        return jnp.concatenate([(qf * m_lo).astype(BF16), (qf * m_hi).astype(BF16)], axis=0)

    lv = lam_ref[...]
    lam = (jnp.exp(jnp.sum(lv[0:1] * lv[1:2], keepdims=True))
           - jnp.exp(jnp.sum(lv[2:3] * lv[3:4], keepdims=True)) + lam_init)
    dng = dng_ref[...]

    for h in range(DIFF_HEADS):
        sl = slice(h * LANES, (h + 1) * LANES)
        s = _dot_nt(split_halves(dq_ref[0, :, sl]), k_ref[0, :, sl])
        e = jnp.exp2(s - jnp.max(s, -1, keepdims=True))
        a = e / jnp.sum(e, -1, keepdims=True)
        w = (a[:tq] - lam * a[tq:]).astype(BF16)
        o = _dot(w, v_ref[0, :, sl])
        o = o * lax.rsqrt(jnp.mean(o * o, -1, keepdims=True) + LN_EPS) * dng * (1.0 - lam_init)
        do_ref[0, :, sl] = o.astype(BF16)

    for hk in range(GQA_KV_HEADS):
        ksl = slice(hk * LANES, (hk + 1) * LANES)
        for m in range(2):
            c0 = (hk * 2 + m) * LANES
            sl = slice(c0, c0 + LANES)
            s = _dot_nt(split_halves(gq_ref[0, :, sl]), gk_ref[0, :, ksl])
            e = jnp.exp2(s - jnp.max(s, -1, keepdims=True))
            l = jnp.sum(e, -1, keepdims=True)
            o = _dot(e.astype(BF16), gv_ref[0, :, ksl]) / l
            go_ref[0, :, sl] = jnp.where(is_lo, o[:tq], o[tq:]).astype(BF16)


def _attn_call(dq, k, v, gq, gk, gv, lamv, dng, lam_init, tq):
    b, s, w = dq.shape
    sk = k.shape[1]
    qspec = pl.BlockSpec((1, tq, w), lambda bi, i: (bi, i, 0))
    kvspec = lambda width: pl.BlockSpec((1, sk, width), lambda bi, i: (bi, 0, 0))
    full = lambda shape: pl.BlockSpec(shape, lambda bi, i: (0,) * len(shape))
    return pl.pallas_call(
        functools.partial(_attn_kernel, lam_init=lam_init),
        grid=(b, s // tq),
        in_specs=[qspec, kvspec(512), kvspec(512), qspec, kvspec(256), kvspec(256),
                  full(lamv.shape), full(dng.shape)],
        out_specs=[qspec, qspec],
        out_shape=[jax.ShapeDtypeStruct((b, s, w), BF16)] * 2,
        compiler_params=_cparams(("arbitrary", "arbitrary")),
        name="attention",
    )(dq, k, v, gq, gk, gv, lamv, dng)


def _fourier_kernel(wc_ref, ws_ref, z_ref, cc_ref, sc_ref, o_ref, *, scale):
    z = z_ref[0]
    p = _dot(wc_ref[...], z).astype(BF16)
    q = _dot(ws_ref[...], z).astype(BF16)
    o_ref[0] = ((_dot(p, cc_ref[...]) - _dot(q, sc_ref[...])) * scale).astype(BF16)


def _fourier_call(z, wc, ws, cc, sc, tm):
    b, s, w = z.shape
    scale = 1.0 / math.sqrt(s * FOURIER_GROUP_DIM)
    return pl.pallas_call(
        functools.partial(_fourier_kernel, scale=scale),
        grid=(s // tm, b),
        in_specs=[pl.BlockSpec((tm, s), lambda i, bi: (i, 0)),
                  pl.BlockSpec((tm, s), lambda i, bi: (i, 0)),
                  pl.BlockSpec((1, s, w), lambda i, bi: (bi, 0, 0)),
                  pl.BlockSpec((w, w), lambda i, bi: (0, 0)),
                  pl.BlockSpec((w, w), lambda i, bi: (0, 0))],
        out_specs=pl.BlockSpec((1, tm, w), lambda i, bi: (bi, i, 0)),
        out_shape=jax.ShapeDtypeStruct((b, s, w), BF16),
        compiler_params=_cparams(("arbitrary", "arbitrary")),
        name="fourier",
    )(wc, ws, z, cc, sc)


def _dft_tables(n):
    k = jnp.arange(n, dtype=jnp.int32)
    m = (k[:, None] * k[None, :]) % n
    ang = m.astype(F32) * (2.0 * math.pi / n)
    return jnp.cos(ang), jnp.sin(ang)


def _merge_kernel(d_ref, g_ref, f_ref, gate_ref, x_ref, mod_ref, wb_ref, wo_ref, lng_ref, lnb_ref, o_ref):
    d = D_MODEL
    m = (gate_ref[:, 0:d] * _dot(d_ref[...], wb_ref[0])
         + gate_ref[:, d:2 * d] * _dot(g_ref[...], wb_ref[1])
         + gate_ref[:, 2 * d:3 * d] * _dot(f_ref[...], wb_ref[2]))
    mix = _dot(m.astype(BF16), wo_ref[...])
    g1 = mod_ref[0][:, 2 * d:3 * d]
    u = DEEPNORM_ALPHA * x_ref[...] + g1 * mix
    o_ref[...] = _ln(u) * lng_ref[...] + lnb_ref[...]


def _merge_call(d_o, g_o, f_o, gates, x, mod3, mod_idx, wb, wo, lng, lnb, tt):
    t, d = x.shape
    full = lambda shape: pl.BlockSpec(shape, lambda i: (0,) * len(shape))
    tok = lambda w: pl.BlockSpec((tt, w), lambda i: (i, 0))
    return pl.pallas_call(
        _merge_kernel,
        grid=(t // tt,),
        in_specs=[tok(512), tok(512), tok(512), tok(N_BRANCH * d), tok(d),
                  pl.BlockSpec((1, 1, mod3.shape[2]), lambda i: (mod_idx(i), 0, 0)),
                  full(wb.shape), full(wo.shape), full(lng.shape), full(lnb.shape)],
        out_specs=tok(d),
        out_shape=jax.ShapeDtypeStruct((t, d), F32),
        compiler_params=_cparams(("arbitrary",)),
        name="merge",
    )(d_o, g_o, f_o, gates, x, mod3, wb, wo, lng, lnb)


N_TOP = PEER_TOPK + 1
PAD_TOP = 24
N_CAND = PAD_TOP + 7 * 8 + (PAD_TOP - 8)


def _top_values(x, n):
    vals = []
    for _ in range(n):
        m = jnp.max(x, axis=0, keepdims=True)
        vals.append(m)
        x = jnp.where(x >= m, NEG_INF, x)
    return vals


def _route_head(s0, s1, av_ref, bv_ref, cand_ref):
    a = _top_values(s0, N_TOP)
    b = _top_values(s1, N_TOP)
    for k in range(N_TOP):
        av_ref[k:k + 1, :] = a[k]
        bv_ref[k:k + 1, :] = b[k]
    av = av_ref[...]
    bv = bv_ref[...]
    cand_ref[0:PAD_TOP, :] = a[0] + bv
    for k in range(1, 8):
        cand_ref[PAD_TOP + 8 * (k - 1):PAD_TOP + 8 * k, :] = a[k] + bv[0:8]
    cand_ref[PAD_TOP + 56:N_CAND, :] = av[8:PAD_TOP] + b[0]
    cand = cand_ref[...]
    c = _top_values(cand, N_TOP)
    tau = 0.5 * (c[PEER_TOPK - 1] + c[PEER_TOPK])
    z = jnp.sum(jnp.where(cand >= tau, jnp.exp(cand - c[0]), 0.0), axis=0, keepdims=True)
    w0 = jnp.exp(s0 - a[0]) / z
    w1 = jnp.exp(s1 - b[0])
    return -s0, s1 - tau, w0, w1


def _route_kernel(x_ref, mod_ref, wq_ref, sk_ref, hb_ref, ns0_ref, a1_ref, w0_ref, w1_ref,
                  av_ref, bv_ref, cand_ref):
    d = D_MODEL
    tt = x_ref.shape[0]
    mod = mod_ref[0]
    sh2 = mod[:, 3 * d:4 * d]
    sc2 = mod[:, 4 * d:5 * d]
    h = _ln(x_ref[...]) * (1.0 + sc2) + sh2
    hb = h.astype(BF16)
    ht = h.T.astype(BF16)
    for cb in range(tt // MM_COLS):
        hb_ref[cb] = ht[:, cb * MM_COLS:(cb + 1) * MM_COLS]
    q = _dot(hb, wq_ref[...])
    lane = lax.broadcasted_iota(jnp.int32, (tt, LANES), 1)
    m_lo = (lane < PEER_KEYS // 2).astype(F32)
    m_hi = 1.0 - m_lo
    av_ref[...] = jnp.full(av_ref.shape, NEG_INF, F32)
    bv_ref[...] = jnp.full(bv_ref.shape, NEG_INF, F32)
    for h in range(PEER_HEADS):
        qh = q[:, h * LANES:(h + 1) * LANES]
        sk = sk_ref[h]
        s0 = _dot_nt(sk, (qh * m_lo).astype(BF16))
        s1 = _dot_nt(sk, (qh * m_hi).astype(BF16))
        ns0, a1, w0, w1 = _route_head(s0, s1, av_ref, bv_ref, cand_ref)
        for tc in range(tt // LANES):
            cols = slice(tc * LANES, (tc + 1) * LANES)
            ns0_ref[h, tc] = ns0[:, cols]
            a1_ref[h, tc] = a1[:, cols]
            w0_ref[h, tc] = w0[:, cols]
            w1_ref[h, tc] = w1[:, cols]


def _route_call(x, mod3, mod_idx, wq, skc, tt):
    t, d = x.shape
    full = lambda shape: pl.BlockSpec(shape, lambda i: (0,) * len(shape))
    rspec = pl.BlockSpec((PEER_HEADS, tt // LANES, PEER_KEYS, LANES), lambda i: (0, i, 0, 0))
    rshape = jax.ShapeDtypeStruct((PEER_HEADS, t // LANES, PEER_KEYS, LANES), F32)
    return pl.pallas_call(
        _route_kernel,
        grid=(t // tt,),
        in_specs=[pl.BlockSpec((tt, d), lambda i: (i, 0)),
                  pl.BlockSpec((1, 1, mod3.shape[2]), lambda i: (mod_idx(i), 0, 0)),
                  full(wq.shape), full(skc.shape)],
        out_specs=[pl.BlockSpec((tt // MM_COLS, d, MM_COLS), lambda i: (i, 0, 0)), rspec, rspec, rspec, rspec],
        out_shape=[jax.ShapeDtypeStruct((t // MM_COLS, d, MM_COLS), BF16), rshape, rshape, rshape, rshape],
        scratch_shapes=[pltpu.VMEM((PAD_TOP, tt), F32), pltpu.VMEM((PAD_TOP, tt), F32),
                        pltpu.VMEM((N_CAND, tt), F32)],
        compiler_params=_cparams(("arbitrary",)),
        name="peer_route",
    )(x, mod3, wq, skc)


MM_COLS = 256
G_ROWS = 64
PEER_CHUNK = 2048


def _peer_kernel(hb_ref, ns0_ref, a1_ref, w0_ref, w1_ref, u_ref, vt_ref, x_ref, mod_ref, lng_ref, lnb_ref,
                 o_ref, acc_ref, a_ref, ga_ref, *, ni):
    c = pl.program_id(1)
    n_cb = hb_ref.shape[0]
    n_tc = ns0_ref.shape[1]

    @pl.when(c == 0)
    def _():
        acc_ref[...] = jnp.zeros_like(acc_ref)

    for cb in range(n_cb):
        a_ref[:, cb * MM_COLS:(cb + 1) * MM_COLS] = _dot(u_ref[...], hb_ref[cb])

    i_base = pl.multiple_of(c * ni, ni)
    for tc in range(n_tc):
        cols = slice(tc * LANES, (tc + 1) * LANES)
        nrows = [ns0_ref[h, tc, pl.ds(i_base, ni), :] for h in range(PEER_HEADS)]
        wrows = [w0_ref[h, tc, pl.ds(i_base, ni), :] for h in range(PEER_HEADS)]
        for ii in range(ni):
            for jh in range(PEER_KEYS // G_ROWS):
                jrows = slice(jh * G_ROWS, (jh + 1) * G_ROWS)
                rows = slice(ii * PEER_KEYS + jh * G_ROWS, ii * PEER_KEYS + (jh + 1) * G_ROWS)
                g = None
                for h in range(PEER_HEADS):
                    term = jnp.where(a1_ref[h, tc, jrows, :] >= nrows[h][ii:ii + 1],
                                     w1_ref[h, tc, jrows, :] * wrows[h][ii:ii + 1], 0.0)
                    g = term if g is None else g + term
                a = a_ref[rows, cols]
                act = 0.5 * a * (1.0 + lax.erf(a * SQRT_HALF))
                ga_ref[rows, cols] = (g * act).astype(BF16)

    acc_ref[...] += _dot(vt_ref[...], ga_ref[...])

    @pl.when(c == pl.num_programs(1) - 1)
    def _():
        d = D_MODEL
        g2 = mod_ref[0][:, 5 * d:6 * d]
        u = DEEPNORM_ALPHA * x_ref[...] + g2 * acc_ref[...].T
        o_ref[...] = _ln(u) * lng_ref[...] + lnb_ref[...]


def _peer_call(hb, ns0, a1, w0, w1, u_tab, vt_tab, x, mod3, mod_idx, lng, lnb, tt, ec):
    t, d = x.shape
    ne = u_tab.shape[0]
    ni = ec // PEER_KEYS
    full = lambda shape: pl.BlockSpec(shape, lambda i, c: (0,) * len(shape))
    rspec = pl.BlockSpec((PEER_HEADS, tt // LANES, PEER_KEYS, LANES), lambda i, c: (0, i, 0, 0))
    return pl.pallas_call(
        functools.partial(_peer_kernel, ni=ni),
        grid=(t // tt, ne // ec),
        in_specs=[pl.BlockSpec((tt // MM_COLS, d, MM_COLS), lambda i, c: (i, 0, 0)), rspec, rspec, rspec, rspec,
                  pl.BlockSpec((ec, d), lambda i, c: (c, 0)),
                  pl.BlockSpec((d, ec), lambda i, c: (0, c)),
                  pl.BlockSpec((tt, d), lambda i, c: (i, 0)),
                  pl.BlockSpec((1, 1, mod3.shape[2]), lambda i, c: (mod_idx(i), 0, 0)),
                  full(lng.shape), full(lnb.shape)],
        out_specs=pl.BlockSpec((tt, d), lambda i, c: (i, 0)),
        out_shape=jax.ShapeDtypeStruct((t, d), F32),
        scratch_shapes=[pltpu.VMEM((d, tt), F32), pltpu.VMEM((ec, tt), F32), pltpu.VMEM((ec, tt), BF16)],
        compiler_params=_cparams(("arbitrary", "arbitrary")),
        name="peer_experts",
    )(hb, ns0, a1, w0, w1, u_tab, vt_tab, x, mod3, lng, lnb)


def _pick_tile(n, want):
    t = min(n, want)
    while n % t:
        t //= 2
    return t


def _rope_tables(seq):
    n_rows = seq // GRID_W
    row = jnp.repeat(jnp.arange(n_rows), GRID_W).astype(F32)
    col = jnp.tile(jnp.arange(GRID_W), n_rows).astype(F32)
    n_freq = DIFF_HEAD_DIM // 4
    freqs = ROPE_THETA ** (-jnp.arange(n_freq, dtype=F32) / n_freq)
    ang = jnp.concatenate([row[:, None] * freqs, col[:, None] * freqs], -1)
    cos = jnp.repeat(jnp.cos(ang), 2, axis=-1)
    sin = jnp.repeat(jnp.sin(ang), 2, axis=-1)
    sign = jnp.where(jnp.arange(DIFF_HEAD_DIM) % 2 == 0, -1.0, 1.0).astype(F32)
    reps = LANES // DIFF_HEAD_DIM
    return jnp.tile(cos, (1, reps)), jnp.tile(sin * sign, (1, reps))


def _dup_heads(a):
    b, s, _ = a.shape
    a4 = a.reshape(b, s, GQA_KV_HEADS, 1, GQA_HEAD_DIM)
    return jnp.broadcast_to(a4, (b, s, GQA_KV_HEADS, 2, GQA_HEAD_DIM)).reshape(b, s, 4 * GQA_HEAD_DIM)


def _layer(x, batch, seq, mod3, mod_idx_fn, lw, rope_tabs, ctx_kv, dft, emit_kv):
    t = batch * seq
    tile = lambda want: _pick_tile(seq if rope_tabs is not None else t, want)
    tt = tile(256)
    mod_idx = lambda i: mod_idx_fn(i, max(seq // tt, 1))
    outs = _in_proj_call(x, mod3, mod_idx, lw["w_in"], lw["qg"], lw["kg"], lw["bd"], rope_tabs, seq, tt, emit_kv)
    dq, dk, dv, gq, gk, gv, fz, gates = outs[:8]
    r3 = lambda a: a.reshape(batch, seq, a.shape[-1])
    k_att, v_att, gk_att, gv_att = r3(dk), r3(dv), r3(gk), r3(gv)
    if ctx_kv is not None:
        cdk, cdv, cgk, cgv = ctx_kv
        k_att = jnp.concatenate([k_att, cdk], axis=1)
        v_att = jnp.concatenate([v_att, cdv], axis=1)
        gk_att = jnp.concatenate([gk_att, cgk], axis=1)
        gv_att = jnp.concatenate([gv_att, cgv], axis=1)
    tq = _pick_tile(seq, 128 if k_att.shape[1] > 1024 else 256)
    d_o, g_o = _attn_call(r3(dq), k_att, v_att, r3(gq), _dup_heads(gk_att), _dup_heads(gv_att),
                          lw["diff_lam"], lw["dng"], lw["lam_init"], tq)
    f_o = _fourier_call(r3(fz), dft[0], dft[1], lw["cc"], lw["sc"], _pick_tile(seq, 512))
    flat = lambda a: a.reshape(t, a.shape[-1])
    tm = tile(512)
    mod_idx_m = lambda i: mod_idx_fn(i, max(seq // tm, 1))
    x1 = _merge_call(flat(d_o), flat(g_o), flat(f_o), gates, x, mod3, mod_idx_m, lw["wb"], lw["wo"],
                     lw["lng0"], lw["lnb0"], tm)
    tr = tile(256)
    mod_idx_r = lambda i: mod_idx_fn(i, max(seq // tr, 1))
    hb, ns0, a1, w0, w1 = _route_call(x1, mod3, mod_idx_r, lw["wq"], lw["skc"], tr)
    x2 = _peer_call(hb, ns0, a1, w0, w1, lw["u"], lw["vt"], x1, mod3, mod_idx_m, lw["lng1"], lw["lnb1"], tm, PEER_CHUNK)
    return x2, outs[8:]


def kernel(x_prompt, x_sample, cache_diff_k, cache_diff_v, cache_gqa_k, cache_gqa_v, c, c_ctx, w_mod, b_mod, w_in,
           diff_lam, diff_norm_g, q_norm_g, k_norm_g, w_branch, w_out, ln_g, ln_b, peer_wq, peer_subkeys, peer_u,
           peer_v):
    depth = w_in.shape[0]
    batch, seq, d = x_prompt.shape
    dbatch, dseq, _ = x_sample.shape
    past = cache_diff_k.shape[2]

    rows = ((dbatch + 1 + 7) // 8) * 8
    cs = jnp.zeros((rows, d), F32).at[:dbatch].set(c).at[dbatch].set(c_ctx)
    mod_all = _mod_call(cs, w_mod, b_mod)

    w_in_b = w_in.astype(BF16)
    wb_b = w_branch.astype(BF16)
    wo_b = w_out.astype(BF16)
    wq_b = peer_wq.astype(BF16)
    u_b = peer_u.astype(BF16)
    vt_b = jnp.swapaxes(peer_v, 1, 2).astype(BF16)
    skc = jnp.transpose(peer_subkeys, (0, 1, 3, 2, 4)).reshape(depth, PEER_HEADS, PEER_KEYS, PEER_KEYS).astype(BF16)
    bd = jnp.asarray(np.kron(np.eye(8, dtype=np.float32), np.full((64, 64), 1.0 / 64, np.float32)), BF16)
    ck, sk_ = _dft_tables(FOURIER_GROUP_DIM)
    eye_g = jnp.eye(FOURIER_GROUPS, dtype=F32)
    cc = jnp.kron(eye_g, ck).astype(BF16)
    sc = jnp.kron(eye_g, sk_).astype(BF16)
    dft_ctx = tuple(a.astype(BF16) for a in _dft_tables(seq))
    dft_lat = tuple(a.astype(BF16) for a in _dft_tables(dseq))
    rope_tabs = _rope_tables(dseq)

    def layer_weights(l):
        return {"w_in": w_in_b[l], "qg": jnp.tile(q_norm_g[l], 8)[None, :], "kg": jnp.tile(k_norm_g[l], 2)[None, :],
                "bd": bd, "diff_lam": diff_lam[l], "dng": diff_norm_g[l][None, :],
                "lam_init": 0.8 - 0.6 * math.exp(-0.3 * l), "cc": cc, "sc": sc, "wb": wb_b[l], "wo": wo_b[l],
                "lng0": ln_g[l, 0][None, :], "lnb0": ln_b[l, 0][None, :], "lng1": ln_g[l, 1][None, :],
                "lnb1": ln_b[l, 1][None, :], "wq": wq_b[l], "skc": skc[l], "u": u_b[l], "vt": vt_b[l]}

    xp = x_prompt.reshape(batch * seq, d)
    ctx_idx = lambda i, bps: dbatch
    kv = [[], [], [], []]
    for l in range(depth):
        mod3 = mod_all[l].reshape(rows, 1, 6 * d)
        xp, own = _layer(xp, batch, seq, mod3, ctx_idx, layer_weights(l), None, None, dft_ctx, True)
        for lst, a in zip(kv, own):
            lst.append(a)
    new_diff_k = jnp.stack(kv[0], 0).reshape(depth, batch, seq, DIFF_HEADS, 2 * DIFF_HEAD_DIM).swapaxes(0, 1)
    new_diff_v = jnp.stack(kv[1], 0).reshape(depth, batch, seq, DIFF_HEADS, 2 * DIFF_HEAD_DIM).swapaxes(0, 1)
    new_gqa_k = jnp.stack(kv[2], 0).reshape(depth, batch, seq, GQA_KV_HEADS, GQA_HEAD_DIM).swapaxes(0, 1)
    new_gqa_v = jnp.stack(kv[3], 0).reshape(depth, batch, seq, GQA_KV_HEADS, GQA_HEAD_DIM).swapaxes(0, 1)

    xs = x_sample.reshape(dbatch * dseq, d)
    lat_idx = lambda i, bps: i // bps
    for l in range(depth):
        mod3 = mod_all[l].reshape(rows, 1, 6 * d)
        ctx_kv = (cache_diff_k[:, l].reshape(dbatch, past, 512).astype(BF16),
                  cache_diff_v[:, l].reshape(dbatch, past, 512).astype(BF16),
                  cache_gqa_k[:, l].reshape(dbatch, past, 128).astype(BF16),
                  cache_gqa_v[:, l].reshape(dbatch, past, 128).astype(BF16))
        xs, _ = _layer(xs, dbatch, dseq, mod3, lat_idx, layer_weights(l), rope_tabs, ctx_kv, dft_lat, False)

    return (xp.reshape(batch, seq, d), xs.reshape(dbatch, dseq, d), new_diff_k, new_diff_v, new_gqa_k, new_gqa_v)
```

```python
import functools
import math

import numpy as np
import jax
import jax.numpy as jnp
from jax import lax
from jax.experimental import pallas as pl
from jax.experimental.pallas import tpu as pltpu

F32 = jnp.float32
BF16 = jnp.bfloat16

D_MODEL = 1024
GRID_W = 64
ROPE_THETA = 10000.0
DIFF_HEADS = 4
DIFF_HEAD_DIM = 64
GQA_KV_HEADS = 2
GQA_HEAD_DIM = 64
FOURIER_GROUP_DIM = 128
FOURIER_GROUPS = 4
N_BRANCH = 3
BRANCH_WIDTH = 512
C_DQ, C_DK, C_DV, C_GQ, C_GK, C_GV, C_FZ, C_GZ, C_END = 0, 512, 1024, 1536, 2048, 2176, 2304, 2816, 5888
PEER_HEADS = 8
PEER_KEYS = 128
PEER_EXPERTS = PEER_KEYS * PEER_KEYS
PEER_TOPK = 16
DEPTH_FOR_NORM = 4
DEEPNORM_ALPHA = (2 * DEPTH_FOR_NORM) ** 0.25
LN_EPS = 1e-6
ATTN_SCALE = 0.125 * math.log2(math.e)
SQRT_HALF = math.sqrt(0.5)

LANES = 128
VMEM_LIMIT = 56 * 1024 * 1024
NEG_INF = float("-inf")


def _cparams(sem):
    return pltpu.CompilerParams(dimension_semantics=sem, vmem_limit_bytes=VMEM_LIMIT)


def _ln(x):
    mu = jnp.mean(x, -1, keepdims=True)
    xc = x - mu
    return xc * lax.rsqrt(jnp.mean(xc * xc, -1, keepdims=True) + LN_EPS)


def _dot(a, b):
    return jnp.dot(a, b, preferred_element_type=F32)


def _dot_nt(a, b):
    return lax.dot_general(a, b, (((1,), (1,)), ((), ())), preferred_element_type=F32)


def _mod_kernel(c_ref, w_ref, b_ref, o_ref):
    c = c_ref[...]
    a = c * jax.nn.sigmoid(c)
    o_ref[0] = jnp.dot(a, w_ref[0], preferred_element_type=F32, precision=lax.Precision.HIGHEST) + b_ref[0]


def _mod_call(cs, w_mod, b_mod):
    depth, d, n = w_mod.shape
    rows = cs.shape[0]
    nt = 1536
    return pl.pallas_call(
        _mod_kernel,
        grid=(depth, n // nt),
        in_specs=[pl.BlockSpec((rows, d), lambda l, j: (0, 0)),
                  pl.BlockSpec((1, d, nt), lambda l, j: (l, 0, j)),
                  pl.BlockSpec((1, 1, nt), lambda l, j: (l, 0, j))],
        out_specs=pl.BlockSpec((1, rows, nt), lambda l, j: (l, 0, j)),
        out_shape=jax.ShapeDtypeStruct((depth, rows, n), F32),
        compiler_params=_cparams(("arbitrary", "arbitrary")),
        name="mod_proj",
    )(cs, w_mod, b_mod.reshape(depth, 1, n))


def _rms64(y, bd, g):
    y2 = y * y
    hi = y2.astype(BF16)
    lo = (y2 - hi.astype(F32)).astype(BF16)
    ms = _dot(hi, bd) + _dot(lo, bd)
    return y * lax.rsqrt(ms + LN_EPS) * g


def _rope(y, cos, sin_signed):
    w = y.shape[1]
    reps = w // LANES
    c = jnp.concatenate([cos] * reps, axis=1) if reps > 1 else cos
    s = jnp.concatenate([sin_signed] * reps, axis=1) if reps > 1 else sin_signed
    lane = lax.broadcasted_iota(jnp.int32, y.shape, 1)
    even = (lane & 1) == 0
    partner = jnp.where(even, pltpu.roll(y, w - 1, 1), pltpu.roll(y, 1, 1))
    return y * c + partner * s


def _in_proj_kernel(*refs, rope, emit_kv):
    x_ref, mod_ref, w_ref, qg_ref, kg_ref, bd_ref = refs[:6]
    pos = 6
    if rope:
        cos_ref, sin_ref = refs[6:8]
        pos = 8
    dq_ref, dk_ref, dv_ref, gq_ref, gk_ref, gv_ref, fz_ref, gate_ref = refs[pos:pos + 8]
    pos += 8
    if emit_kv:
        ndk_ref, ndv_ref, ngk_ref, ngv_ref = refs[pos:pos + 4]

    mod = mod_ref[0]
    sh1 = mod[:, 0:D_MODEL]
    sc1 = mod[:, D_MODEL:2 * D_MODEL]
    h = (_ln(x_ref[...]) * (1.0 + sc1) + sh1).astype(BF16)

    def seg(lo, hi):
        return _dot(h, w_ref[:, lo:hi])

    if rope:
        cos = cos_ref[...]
        sin = sin_ref[...]
        rot = lambda y: _rope(y, cos, sin)
    else:
        rot = lambda y: y

    dq_ref[...] = (rot(seg(C_DQ, C_DK)) * ATTN_SCALE).astype(BF16)
    dk = seg(C_DK, C_DV)
    dk_ref[...] = rot(dk).astype(BF16)
    dv = seg(C_DV, C_GQ)
    dv_ref[...] = dv.astype(BF16)
    gq = _rms64(seg(C_GQ, C_GK), bd_ref[...], qg_ref[...])
    gq_ref[...] = (rot(gq) * ATTN_SCALE).astype(BF16)
    gk = _rms64(seg(C_GK, C_GV), bd_ref[0:LANES, 0:LANES], kg_ref[...])
    gk_ref[...] = rot(gk).astype(BF16)
    gv = seg(C_GV, C_FZ)
    gv_ref[...] = gv.astype(BF16)
    fz_ref[...] = seg(C_FZ, C_GZ).astype(BF16)
    for n in range(N_BRANCH):
        lo = C_GZ + n * D_MODEL
        gate_ref[:, n * D_MODEL:(n + 1) * D_MODEL] = jax.nn.sigmoid(seg(lo, lo + D_MODEL))
    if emit_kv:
        ndk_ref[...] = dk
        ndv_ref[...] = dv
        ngk_ref[...] = gk
        ngv_ref[...] = gv


def _in_proj_call(x, mod3, mod_idx, w_in, qg, kg, bd, rope_tabs, seq, tt, emit_kv):
    t, d = x.shape
    rope = rope_tabs is not None
    nblk_seq = seq // tt
    full = lambda shape: pl.BlockSpec(shape, lambda i: (0,) * len(shape))
    tok = lambda w: pl.BlockSpec((tt, w), lambda i: (i, 0))
    in_specs = [tok(d),
                pl.BlockSpec((1, 1, mod3.shape[2]), lambda i: (mod_idx(i), 0, 0)),
                pl.BlockSpec(w_in.shape, lambda i: (0, 0), pipeline_mode=pl.Buffered(1)),
                full(qg.shape), full(kg.shape), full(bd.shape)]
    args = [x, mod3, w_in, qg, kg, bd]
    if rope:
        in_specs += [pl.BlockSpec((tt, LANES), lambda i: (i % nblk_seq, 0))] * 2
        args += list(rope_tabs)
    widths = [512, 512, 512, 512, 128, 128, 512]
    out_specs = [tok(w) for w in widths] + [tok(N_BRANCH * d)]
    out_shape = [jax.ShapeDtypeStruct((t, w), BF16) for w in widths] + [jax.ShapeDtypeStruct((t, N_BRANCH * d), F32)]
    if emit_kv:
        for w in (512, 512, 128, 128):
            out_specs.append(tok(w))
            out_shape.append(jax.ShapeDtypeStruct((t, w), F32))
    return pl.pallas_call(
        functools.partial(_in_proj_kernel, rope=rope, emit_kv=emit_kv),
        grid=(t // tt,),
        in_specs=in_specs,
        out_specs=out_specs,
        out_shape=out_shape,
        compiler_params=_cparams(("arbitrary",)),
        name="in_proj",
    )(*args)


def _attn_kernel(dq_ref, k_ref, v_ref, gq_ref, gk_ref, gv_ref, lam_ref, dng_ref, do_ref, go_ref, *, lam_init):
    tq = dq_ref.shape[1]
    lane = lax.broadcasted_iota(jnp.int32, (tq, LANES), 1)
    is_lo = lane < DIFF_HEAD_DIM
    m_lo = is_lo.astype(F32)
    m_hi = 1.0 - m_lo

    def split_halves(q):
        qf = q.astype(F32)
        return jnp.concatenate([(qf * m_lo).astype(BF16), (qf * m_hi).astype(BF16)], axis=0)

    lv = lam_ref[...]
    lam = (jnp.exp(jnp.sum(lv[0:1] * lv[1:2], keepdims=True))
           - jnp.exp(jnp.sum(lv[2:3] * lv[3:4], keepdims=True)) + lam_init)
    dng = dng_ref[...]

    for h in range(DIFF_HEADS):
        sl = slice(h * LANES, (h + 1) * LANES)
        s = _dot_nt(split_halves(dq_ref[0, :, sl]), k_ref[0, :, sl])
        e = jnp.exp2(s - jnp.max(s, -1, keepdims=True))
        a = e / jnp.sum(e, -1, keepdims=True)
        w = (a[:tq] - lam * a[tq:]).astype(BF16)
        o = _dot(w, v_ref[0, :, sl])
        o = o * lax.rsqrt(jnp.mean(o * o, -1, keepdims=True) + LN_EPS) * dng * (1.0 - lam_init)
        do_ref[0, :, sl] = o.astype(BF16)

    for hk in range(GQA_KV_HEADS):
        ksl = slice(hk * LANES, (hk + 1) * LANES)
        for m in range(2):
            c0 = (hk * 2 + m) * LANES
            sl = slice(c0, c0 + LANES)
            s = _dot_nt(split_halves(gq_ref[0, :, sl]), gk_ref[0, :, ksl])
            e = jnp.exp2(s - jnp.max(s, -1, keepdims=True))
            l = jnp.sum(e, -1, keepdims=True)
            o = _dot(e.astype(BF16), gv_ref[0, :, ksl]) / l
            go_ref[0, :, sl] = jnp.where(is_lo, o[:tq], o[tq:]).astype(BF16)


def _attn_call(dq, k, v, gq, gk, gv, lamv, dng, lam_init, tq):
    b, s, w = dq.shape
    sk = k.shape[1]
    qspec = pl.BlockSpec((1, tq, w), lambda bi, i: (bi, i, 0))
    kvspec = lambda width: pl.BlockSpec((1, sk, width), lambda bi, i: (bi, 0, 0))
    full = lambda shape: pl.BlockSpec(shape, lambda bi, i: (0,) * len(shape))
    return pl.pallas_call(
        functools.partial(_attn_kernel, lam_init=lam_init),
        grid=(b, s // tq),
        in_specs=[qspec, kvspec(512), kvspec(512), qspec, kvspec(256), kvspec(256),
                  full(lamv.shape), full(dng.shape)],
        out_specs=[qspec, qspec],
        out_shape=[jax.ShapeDtypeStruct((b, s, w), BF16)] * 2,
        compiler_params=_cparams(("arbitrary", "arbitrary")),
        name="attention",
    )(dq, k, v, gq, gk, gv, lamv, dng)


def _fourier_kernel(wc_ref, ws_ref, z_ref, cc_ref, sc_ref, o_ref, *, scale):
    z = z_ref[0]
    p = _dot(wc_ref[...], z).astype(BF16)
    q = _dot(ws_ref[...], z).astype(BF16)
    o_ref[0] = ((_dot(p, cc_ref[...]) - _dot(q, sc_ref[...])) * scale).astype(BF16)


def _fourier_call(z, wc, ws, cc, sc, tm):
    b, s, w = z.shape
    scale = 1.0 / math.sqrt(s * FOURIER_GROUP_DIM)
    return pl.pallas_call(
        functools.partial(_fourier_kernel, scale=scale),
        grid=(s // tm, b),
        in_specs=[pl.BlockSpec((tm, s), lambda i, bi: (i, 0)),
                  pl.BlockSpec((tm, s), lambda i, bi: (i, 0)),
                  pl.BlockSpec((1, s, w), lambda i, bi: (bi, 0, 0)),
                  pl.BlockSpec((w, w), lambda i, bi: (0, 0)),
                  pl.BlockSpec((w, w), lambda i, bi: (0, 0))],
        out_specs=pl.BlockSpec((1, tm, w), lambda i, bi: (bi, i, 0)),
        out_shape=jax.ShapeDtypeStruct((b, s, w), BF16),
        compiler_params=_cparams(("arbitrary", "arbitrary")),
        name="fourier",
    )(wc, ws, z, cc, sc)


def _dft_tables(n):
    k = jnp.arange(n, dtype=jnp.int32)
    m = (k[:, None] * k[None, :]) % n
    ang = m.astype(F32) * (2.0 * math.pi / n)
    return jnp.cos(ang), jnp.sin(ang)


def _merge_kernel(d_ref, g_ref, f_ref, gate_ref, x_ref, mod_ref, wb_ref, wo_ref, lng_ref, lnb_ref, o_ref):
    d = D_MODEL
    m = (gate_ref[:, 0:d] * _dot(d_ref[...], wb_ref[0])
         + gate_ref[:, d:2 * d] * _dot(g_ref[...], wb_ref[1])
         + gate_ref[:, 2 * d:3 * d] * _dot(f_ref[...], wb_ref[2]))
    mix = _dot(m.astype(BF16), wo_ref[...])
    g1 = mod_ref[0][:, 2 * d:3 * d]
    u = DEEPNORM_ALPHA * x_ref[...] + g1 * mix
    o_ref[...] = _ln(u) * lng_ref[...] + lnb_ref[...]


def _merge_call(d_o, g_o, f_o, gates, x, mod3, mod_idx, wb, wo, lng, lnb, tt):
    t, d = x.shape
    full = lambda shape: pl.BlockSpec(shape, lambda i: (0,) * len(shape))
    tok = lambda w: pl.BlockSpec((tt, w), lambda i: (i, 0))
    return pl.pallas_call(
        _merge_kernel,
        grid=(t // tt,),
        in_specs=[tok(512), tok(512), tok(512), tok(N_BRANCH * d), tok(d),
                  pl.BlockSpec((1, 1, mod3.shape[2]), lambda i: (mod_idx(i), 0, 0)),
                  full(wb.shape), full(wo.shape), full(lng.shape), full(lnb.shape)],
        out_specs=tok(d),
        out_shape=jax.ShapeDtypeStruct((t, d), F32),
        compiler_params=_cparams(("arbitrary",)),
        name="merge",
    )(d_o, g_o, f_o, gates, x, mod3, wb, wo, lng, lnb)


N_TOP = PEER_TOPK + 1
PAD_TOP = 24
N_CAND = PAD_TOP + 7 * 8 + (PAD_TOP - 8)


def _top_values(x, n):
    vals = []
    for _ in range(n):
        m = jnp.max(x, axis=0, keepdims=True)
        vals.append(m)
        x = jnp.where(x >= m, NEG_INF, x)
    return vals


def _route_head(s0, s1, av_ref, bv_ref, cand_ref):
    a = _top_values(s0, N_TOP)
    b = _top_values(s1, N_TOP)
    for k in range(N_TOP):
        av_ref[k:k + 1, :] = a[k]
        bv_ref[k:k + 1, :] = b[k]
    av = av_ref[...]
    bv = bv_ref[...]
    cand_ref[0:PAD_TOP, :] = a[0] + bv
    for k in range(1, 8):
        cand_ref[PAD_TOP + 8 * (k - 1):PAD_TOP + 8 * k, :] = a[k] + bv[0:8]
    cand_ref[PAD_TOP + 56:N_CAND, :] = av[8:PAD_TOP] + b[0]
    cand = cand_ref[...]
    c = _top_values(cand, N_TOP)
    tau = 0.5 * (c[PEER_TOPK - 1] + c[PEER_TOPK])
    z = jnp.sum(jnp.where(cand >= tau, jnp.exp(cand - c[0]), 0.0), axis=0, keepdims=True)
    w0 = jnp.exp(s0 - a[0]) / z
    w1 = jnp.exp(s1 - b[0])
    return -s0, s1 - tau, w0, w1


def _route_kernel(x_ref, mod_ref, wq_ref, sk_ref, hb_ref, ns0_ref, a1_ref, w0_ref, w1_ref,
                  av_ref, bv_ref, cand_ref):
    d = D_MODEL
    tt = x_ref.shape[0]
    mod = mod_ref[0]
    sh2 = mod[:, 3 * d:4 * d]
    sc2 = mod[:, 4 * d:5 * d]
    h = _ln(x_ref[...]) * (1.0 + sc2) + sh2
    hb = h.astype(BF16)
    ht = h.T.astype(BF16)
    for cb in range(tt // MM_COLS):
        hb_ref[cb] = ht[:, cb * MM_COLS:(cb + 1) * MM_COLS]
    q = _dot(hb, wq_ref[...])
    lane = lax.broadcasted_iota(jnp.int32, (tt, LANES), 1)
    m_lo = (lane < PEER_KEYS // 2).astype(F32)
    m_hi = 1.0 - m_lo
    av_ref[...] = jnp.full(av_ref.shape, NEG_INF, F32)
    bv_ref[...] = jnp.full(bv_ref.shape, NEG_INF, F32)
    for h in range(PEER_HEADS):
        qh = q[:, h * LANES:(h + 1) * LANES]
        sk = sk_ref[h]
        s0 = _dot_nt(sk, (qh * m_lo).astype(BF16))
        s1 = _dot_nt(sk, (qh * m_hi).astype(BF16))
        ns0, a1, w0, w1 = _route_head(s0, s1, av_ref, bv_ref, cand_ref)
        for tc in range(tt // LANES):
            cols = slice(tc * LANES, (tc + 1) * LANES)
            ns0_ref[h, tc] = ns0[:, cols]
            a1_ref[h, tc] = a1[:, cols]
            w0_ref[h, tc] = w0[:, cols]
            w1_ref[h, tc] = w1[:, cols]


def _route_call(x, mod3, mod_idx, wq, skc, tt):
    t, d = x.shape
    full = lambda shape: pl.BlockSpec(shape, lambda i: (0,) * len(shape))
    rspec = pl.BlockSpec((PEER_HEADS, tt // LANES, PEER_KEYS, LANES), lambda i: (0, i, 0, 0))
    rshape = jax.ShapeDtypeStruct((PEER_HEADS, t // LANES, PEER_KEYS, LANES), F32)
    return pl.pallas_call(
        _route_kernel,
        grid=(t // tt,),
        in_specs=[pl.BlockSpec((tt, d), lambda i: (i, 0)),
                  pl.BlockSpec((1, 1, mod3.shape[2]), lambda i: (mod_idx(i), 0, 0)),
                  full(wq.shape), full(skc.shape)],
        out_specs=[pl.BlockSpec((tt // MM_COLS, d, MM_COLS), lambda i: (i, 0, 0)), rspec, rspec, rspec, rspec],
        out_shape=[jax.ShapeDtypeStruct((t // MM_COLS, d, MM_COLS), BF16), rshape, rshape, rshape, rshape],
        scratch_shapes=[pltpu.VMEM((PAD_TOP, tt), F32), pltpu.VMEM((PAD_TOP, tt), F32),
                        pltpu.VMEM((N_CAND, tt), F32)],
        compiler_params=_cparams(("arbitrary",)),
        name="peer_route",
    )(x, mod3, wq, skc)


MM_COLS = 256
G_ROWS = 32
G_REUSE = 8
PEER_CHUNK = 1024


def _peer_kernel(hb_ref, ns0_ref, a1_ref, w0_ref, w1_ref, u_ref, vt_ref, x_ref, mod_ref, lng_ref, lnb_ref,
                 o_ref, acc_ref, a_ref, ga_ref, *, ni):
    c = pl.program_id(1)
    n_cb = hb_ref.shape[0]
    n_tc = ns0_ref.shape[1]

    @pl.when(c == 0)
    def _():
        acc_ref[...] = jnp.zeros_like(acc_ref)

    for cb in range(n_cb):
        a_blk = _dot(u_ref[...], hb_ref[cb])
        a_ref[2 * cb] = a_blk[:, :LANES]
        a_ref[2 * cb + 1] = a_blk[:, LANES:]

    i_base = pl.multiple_of(c * ni, ni)
    n_jq = PEER_KEYS // G_ROWS

    def gate_tile(it, carry):
        tc = it // n_jq
        j0 = pl.multiple_of((it % n_jq) * G_ROWS, G_ROWS)
        for ig in range(ni // G_REUSE):
            g = [None] * G_REUSE
            for h in range(PEER_HEADS):
                nrows = ns0_ref[h, tc, pl.ds(pl.multiple_of(i_base + ig * G_REUSE, G_REUSE), G_REUSE), :]
                wrows = w0_ref[h, tc, pl.ds(pl.multiple_of(i_base + ig * G_REUSE, G_REUSE), G_REUSE), :]
                a1q = a1_ref[h, tc, pl.ds(j0, G_ROWS), :]
                w1q = w1_ref[h, tc, pl.ds(j0, G_ROWS), :]
                for r in range(G_REUSE):
                    term = jnp.where(a1q >= nrows[r:r + 1], w1q * wrows[r:r + 1], 0.0)
                    g[r] = term if g[r] is None else g[r] + term
            for r in range(G_REUSE):
                row0 = pl.multiple_of((ig * G_REUSE + r) * PEER_KEYS + j0, G_ROWS)
                a = a_ref[tc, pl.ds(row0, G_ROWS), :]
                act = 0.5 * a * (1.0 + lax.erf(a * SQRT_HALF))
                ga_ref[tc, pl.ds(row0, G_ROWS), :] = (g[r] * act).astype(BF16)
        return carry

    lax.fori_loop(0, n_tc * n_jq, gate_tile, 0)

    ga = jnp.concatenate([ga_ref[tc] for tc in range(n_tc)], axis=1)
    acc_ref[...] += _dot(vt_ref[...], ga)

    @pl.when(c == pl.num_programs(1) - 1)
    def _():
        d = D_MODEL
        g2 = mod_ref[0][:, 5 * d:6 * d]
        u = DEEPNORM_ALPHA * x_ref[...] + g2 * acc_ref[...].T
        o_ref[...] = _ln(u) * lng_ref[...] + lnb_ref[...]


def _peer_call(hb, ns0, a1, w0, w1, u_tab, vt_tab, x, mod3, mod_idx, lng, lnb, tt, ec):
    t, d = x.shape
    ne = u_tab.shape[0]
    ni = ec // PEER_KEYS
    full = lambda shape: pl.BlockSpec(shape, lambda i, c: (0,) * len(shape))
    rspec = pl.BlockSpec((PEER_HEADS, tt // LANES, PEER_KEYS, LANES), lambda i, c: (0, i, 0, 0))
    return pl.pallas_call(
        functools.partial(_peer_kernel, ni=ni),
        grid=(t // tt, ne // ec),
        in_specs=[pl.BlockSpec((tt // MM_COLS, d, MM_COLS), lambda i, c: (i, 0, 0)), rspec, rspec, rspec, rspec,
                  pl.BlockSpec((ec, d), lambda i, c: (c, 0)),
                  pl.BlockSpec((d, ec), lambda i, c: (0, c)),
                  pl.BlockSpec((tt, d), lambda i, c: (i, 0)),
                  pl.BlockSpec((1, 1, mod3.shape[2]), lambda i, c: (mod_idx(i), 0, 0)),
                  full(lng.shape), full(lnb.shape)],
        out_specs=pl.BlockSpec((tt, d), lambda i, c: (i, 0)),
        out_shape=jax.ShapeDtypeStruct((t, d), F32),
        scratch_shapes=[pltpu.VMEM((d, tt), F32), pltpu.VMEM((tt // LANES, ec, LANES), F32),
                        pltpu.VMEM((tt // LANES, ec, LANES), BF16)],
        compiler_params=_cparams(("arbitrary", "arbitrary")),
        name="peer_experts",
    )(hb, ns0, a1, w0, w1, u_tab, vt_tab, x, mod3, lng, lnb)


def _pick_tile(n, want):
    t = min(n, want)
    while n % t:
        t //= 2
    return t


def _rope_tables(seq):
    n_rows = seq // GRID_W
    row = jnp.repeat(jnp.arange(n_rows), GRID_W).astype(F32)
    col = jnp.tile(jnp.arange(GRID_W), n_rows).astype(F32)
    n_freq = DIFF_HEAD_DIM // 4
    freqs = ROPE_THETA ** (-jnp.arange(n_freq, dtype=F32) / n_freq)
    ang = jnp.concatenate([row[:, None] * freqs, col[:, None] * freqs], -1)
    cos = jnp.repeat(jnp.cos(ang), 2, axis=-1)
    sin = jnp.repeat(jnp.sin(ang), 2, axis=-1)
    sign = jnp.where(jnp.arange(DIFF_HEAD_DIM) % 2 == 0, -1.0, 1.0).astype(F32)
    reps = LANES // DIFF_HEAD_DIM
    return jnp.tile(cos, (1, reps)), jnp.tile(sin * sign, (1, reps))


def _dup_heads(a):
    b, s, _ = a.shape
    a4 = a.reshape(b, s, GQA_KV_HEADS, 1, GQA_HEAD_DIM)
    return jnp.broadcast_to(a4, (b, s, GQA_KV_HEADS, 2, GQA_HEAD_DIM)).reshape(b, s, 4 * GQA_HEAD_DIM)


def _layer(x, batch, seq, mod3, mod_idx_fn, lw, rope_tabs, ctx_kv, dft, emit_kv):
    t = batch * seq
    tile = lambda want: _pick_tile(seq if rope_tabs is not None else t, want)
    tt = tile(256)
    mod_idx = lambda i: mod_idx_fn(i, max(seq // tt, 1))
    outs = _in_proj_call(x, mod3, mod_idx, lw["w_in"], lw["qg"], lw["kg"], lw["bd"], rope_tabs, seq, tt, emit_kv)
    dq, dk, dv, gq, gk, gv, fz, gates = outs[:8]
    r3 = lambda a: a.reshape(batch, seq, a.shape[-1])
    k_att, v_att, gk_att, gv_att = r3(dk), r3(dv), r3(gk), r3(gv)
    if ctx_kv is not None:
        cdk, cdv, cgk, cgv = ctx_kv
        k_att = jnp.concatenate([k_att, cdk], axis=1)
        v_att = jnp.concatenate([v_att, cdv], axis=1)
        gk_att = jnp.concatenate([gk_att, cgk], axis=1)
        gv_att = jnp.concatenate([gv_att, cgv], axis=1)
    tq = _pick_tile(seq, 128 if k_att.shape[1] > 1024 else 256)
    d_o, g_o = _attn_call(r3(dq), k_att, v_att, r3(gq), _dup_heads(gk_att), _dup_heads(gv_att),
                          lw["diff_lam"], lw["dng"], lw["lam_init"], tq)
    f_o = _fourier_call(r3(fz), dft[0], dft[1], lw["cc"], lw["sc"], _pick_tile(seq, 512))
    flat = lambda a: a.reshape(t, a.shape[-1])
    tm = tile(512)
    mod_idx_m = lambda i: mod_idx_fn(i, max(seq // tm, 1))
    x1 = _merge_call(flat(d_o), flat(g_o), flat(f_o), gates, x, mod3, mod_idx_m, lw["wb"], lw["wo"],
                     lw["lng0"], lw["lnb0"], tm)
    tr = tile(256)
    mod_idx_r = lambda i: mod_idx_fn(i, max(seq // tr, 1))
    hb, ns0, a1, w0, w1 = _route_call(x1, mod3, mod_idx_r, lw["wq"], lw["skc"], tr)
    x2 = _peer_call(hb, ns0, a1, w0, w1, lw["u"], lw["vt"], x1, mod3, mod_idx_m, lw["lng1"], lw["lnb1"], tm, PEER_CHUNK)
    return x2, outs[8:]


def kernel(x_prompt, x_sample, cache_diff_k, cache_diff_v, cache_gqa_k, cache_gqa_v, c, c_ctx, w_mod, b_mod, w_in,
           diff_lam, diff_norm_g, q_norm_g, k_norm_g, w_branch, w_out, ln_g, ln_b, peer_wq, peer_subkeys, peer_u,
           peer_v):
    depth = w_in.shape[0]
    batch, seq, d = x_prompt.shape
    dbatch, dseq, _ = x_sample.shape
    past = cache_diff_k.shape[2]

    rows = ((dbatch + 1 + 7) // 8) * 8
    cs = jnp.zeros((rows, d), F32).at[:dbatch].set(c).at[dbatch].set(c_ctx)
    mod_all = _mod_call(cs, w_mod, b_mod)

    w_in_b = w_in.astype(BF16)
    wb_b = w_branch.astype(BF16)
    wo_b = w_out.astype(BF16)
    wq_b = peer_wq.astype(BF16)
    u_b = peer_u.astype(BF16)
    vt_b = jnp.swapaxes(peer_v, 1, 2).astype(BF16)
    skc = jnp.transpose(peer_subkeys, (0, 1, 3, 2, 4)).reshape(depth, PEER_HEADS, PEER_KEYS, PEER_KEYS).astype(BF16)
    bd = jnp.asarray(np.kron(np.eye(8, dtype=np.float32), np.full((64, 64), 1.0 / 64, np.float32)), BF16)
    ck, sk_ = _dft_tables(FOURIER_GROUP_DIM)
    eye_g = jnp.eye(FOURIER_GROUPS, dtype=F32)
    cc = jnp.kron(eye_g, ck).astype(BF16)
    sc = jnp.kron(eye_g, sk_).astype(BF16)
    dft_ctx = tuple(a.astype(BF16) for a in _dft_tables(seq))
    dft_lat = tuple(a.astype(BF16) for a in _dft_tables(dseq))
    rope_tabs = _rope_tables(dseq)

    def layer_weights(l):
        return {"w_in": w_in_b[l], "qg": jnp.tile(q_norm_g[l], 8)[None, :], "kg": jnp.tile(k_norm_g[l], 2)[None, :],
                "bd": bd, "diff_lam": diff_lam[l], "dng": diff_norm_g[l][None, :],
                "lam_init": 0.8 - 0.6 * math.exp(-0.3 * l), "cc": cc, "sc": sc, "wb": wb_b[l], "wo": wo_b[l],
                "lng0": ln_g[l, 0][None, :], "lnb0": ln_b[l, 0][None, :], "lng1": ln_g[l, 1][None, :],
                "lnb1": ln_b[l, 1][None, :], "wq": wq_b[l], "skc": skc[l], "u": u_b[l], "vt": vt_b[l]}

    xp = x_prompt.reshape(batch * seq, d)
    ctx_idx = lambda i, bps: dbatch
    kv = [[], [], [], []]
    for l in range(depth):
        mod3 = mod_all[l].reshape(rows, 1, 6 * d)
        xp, own = _layer(xp, batch, seq, mod3, ctx_idx, layer_weights(l), None, None, dft_ctx, True)
        for lst, a in zip(kv, own):
            lst.append(a)
    new_diff_k = jnp.stack(kv[0], 0).reshape(depth, batch, seq, DIFF_HEADS, 2 * DIFF_HEAD_DIM).swapaxes(0, 1)
    new_diff_v = jnp.stack(kv[1], 0).reshape(depth, batch, seq, DIFF_HEADS, 2 * DIFF_HEAD_DIM).swapaxes(0, 1)
    new_gqa_k = jnp.stack(kv[2], 0).reshape(depth, batch, seq, GQA_KV_HEADS, GQA_HEAD_DIM).swapaxes(0, 1)
    new_gqa_v = jnp.stack(kv[3], 0).reshape(depth, batch, seq, GQA_KV_HEADS, GQA_HEAD_DIM).swapaxes(0, 1)

    xs = x_sample.reshape(dbatch * dseq, d)
    lat_idx = lambda i, bps: i // bps
    for l in range(depth):
        mod3 = mod_all[l].reshape(rows, 1, 6 * d)
        ctx_kv = (cache_diff_k[:, l].reshape(dbatch, past, 512).astype(BF16),
                  cache_diff_v[:, l].reshape(dbatch, past, 512).astype(BF16),
                  cache_gqa_k[:, l].reshape(dbatch, past, 128).astype(BF16),
                  cache_gqa_v[:, l].reshape(dbatch, past, 128).astype(BF16))
        xs, _ = _layer(xs, dbatch, dseq, mod3, lat_idx, layer_weights(l), rope_tabs, ctx_kv, dft_lat, False)

    return (xp.reshape(batch, seq, d), xs.reshape(dbatch, dseq, d), new_diff_k, new_diff_v, new_gqa_k, new_gqa_v)
```

```python
import functools
import math

import numpy as np
import jax
import jax.numpy as jnp
from jax import lax
from jax.experimental import pallas as pl
from jax.experimental.pallas import tpu as pltpu

F32 = jnp.float32
BF16 = jnp.bfloat16

D_MODEL = 1024
GRID_W = 64
ROPE_THETA = 10000.0
DIFF_HEADS = 4
DIFF_HEAD_DIM = 64
GQA_KV_HEADS = 2
GQA_HEAD_DIM = 64
FOURIER_GROUP_DIM = 128
FOURIER_GROUPS = 4
N_BRANCH = 3
BRANCH_WIDTH = 512
C_DQ, C_DK, C_DV, C_GQ, C_GK, C_GV, C_FZ, C_GZ, C_END = 0, 512, 1024, 1536, 2048, 2176, 2304, 2816, 5888
PEER_HEADS = 8
PEER_KEYS = 128
PEER_EXPERTS = PEER_KEYS * PEER_KEYS
PEER_TOPK = 16
DEPTH_FOR_NORM = 4
DEEPNORM_ALPHA = (2 * DEPTH_FOR_NORM) ** 0.25
LN_EPS = 1e-6
ATTN_SCALE = 0.125 * math.log2(math.e)
SQRT_HALF = math.sqrt(0.5)

LANES = 128
VMEM_LIMIT = 56 * 1024 * 1024
NEG_INF = float("-inf")


def _cparams(sem):
    return pltpu.CompilerParams(dimension_semantics=sem, vmem_limit_bytes=VMEM_LIMIT)


def _ln(x):
    mu = jnp.mean(x, -1, keepdims=True)
    xc = x - mu
    return xc * lax.rsqrt(jnp.mean(xc * xc, -1, keepdims=True) + LN_EPS)


def _dot(a, b):
    return jnp.dot(a, b, preferred_element_type=F32)


def _dot_nt(a, b):
    return lax.dot_general(a, b, (((1,), (1,)), ((), ())), preferred_element_type=F32)


def _mod_kernel(c_ref, w_ref, b_ref, o_ref):
    c = c_ref[...]
    a = c * jax.nn.sigmoid(c)
    o_ref[0] = jnp.dot(a, w_ref[0], preferred_element_type=F32, precision=lax.Precision.HIGHEST) + b_ref[0]


def _mod_call(cs, w_mod, b_mod):
    depth, d, n = w_mod.shape
    rows = cs.shape[0]
    nt = 1536
    return pl.pallas_call(
        _mod_kernel,
        grid=(depth, n // nt),
        in_specs=[pl.BlockSpec((rows, d), lambda l, j: (0, 0)),
                  pl.BlockSpec((1, d, nt), lambda l, j: (l, 0, j)),
                  pl.BlockSpec((1, 1, nt), lambda l, j: (l, 0, j))],
        out_specs=pl.BlockSpec((1, rows, nt), lambda l, j: (l, 0, j)),
        out_shape=jax.ShapeDtypeStruct((depth, rows, n), F32),
        compiler_params=_cparams(("arbitrary", "arbitrary")),
        name="mod_proj",
    )(cs, w_mod, b_mod.reshape(depth, 1, n))


def _rms64(y, bd, g):
    y2 = y * y
    hi = y2.astype(BF16)
    lo = (y2 - hi.astype(F32)).astype(BF16)
    ms = _dot(hi, bd) + _dot(lo, bd)
    return y * lax.rsqrt(ms + LN_EPS) * g


def _rope(y, cos, sin_signed):
    w = y.shape[1]
    reps = w // LANES
    c = jnp.concatenate([cos] * reps, axis=1) if reps > 1 else cos
    s = jnp.concatenate([sin_signed] * reps, axis=1) if reps > 1 else sin_signed
    lane = lax.broadcasted_iota(jnp.int32, y.shape, 1)
    even = (lane & 1) == 0
    partner = jnp.where(even, pltpu.roll(y, w - 1, 1), pltpu.roll(y, 1, 1))
    return y * c + partner * s


def _in_proj_kernel(*refs, rope, emit_kv):
    x_ref, mod_ref, w_ref, qg_ref, kg_ref, bd_ref = refs[:6]
    pos = 6
    if rope:
        cos_ref, sin_ref = refs[6:8]
        pos = 8
    dq_ref, dk_ref, dv_ref, gq_ref, gk_ref, gv_ref, fz_ref, gate_ref = refs[pos:pos + 8]
    pos += 8
    if emit_kv:
        ndk_ref, ndv_ref, ngk_ref, ngv_ref = refs[pos:pos + 4]

    mod = mod_ref[0]
    sh1 = mod[:, 0:D_MODEL]
    sc1 = mod[:, D_MODEL:2 * D_MODEL]
    h = (_ln(x_ref[...]) * (1.0 + sc1) + sh1).astype(BF16)

    def seg(lo, hi):
        return _dot(h, w_ref[:, lo:hi])

    if rope:
        cos = cos_ref[...]
        sin = sin_ref[...]
        rot = lambda y: _rope(y, cos, sin)
    else:
        rot = lambda y: y

    dq_ref[...] = (rot(seg(C_DQ, C_DK)) * ATTN_SCALE).astype(BF16)
    dk = seg(C_DK, C_DV)
    dk_ref[...] = rot(dk).astype(BF16)
    dv = seg(C_DV, C_GQ)
    dv_ref[...] = dv.astype(BF16)
    gq = _rms64(seg(C_GQ, C_GK), bd_ref[...], qg_ref[...])
    gq_ref[...] = (rot(gq) * ATTN_SCALE).astype(BF16)
    gk = _rms64(seg(C_GK, C_GV), bd_ref[0:LANES, 0:LANES], kg_ref[...])
    gk_ref[...] = rot(gk).astype(BF16)
    gv = seg(C_GV, C_FZ)
    gv_ref[...] = gv.astype(BF16)
    fz_ref[...] = seg(C_FZ, C_GZ).astype(BF16)
    for n in range(N_BRANCH):
        lo = C_GZ + n * D_MODEL
        gate_ref[:, n * D_MODEL:(n + 1) * D_MODEL] = jax.nn.sigmoid(seg(lo, lo + D_MODEL))
    if emit_kv:
        ndk_ref[...] = dk
        ndv_ref[...] = dv
        ngk_ref[...] = gk
        ngv_ref[...] = gv


def _in_proj_call(x, mod3, mod_idx, w_in, qg, kg, bd, rope_tabs, seq, tt, emit_kv):
    t, d = x.shape
    rope = rope_tabs is not None
    nblk_seq = seq // tt
    full = lambda shape: pl.BlockSpec(shape, lambda i: (0,) * len(shape))
    tok = lambda w: pl.BlockSpec((tt, w), lambda i: (i, 0))
    in_specs = [tok(d),
                pl.BlockSpec((1, 1, mod3.shape[2]), lambda i: (mod_idx(i), 0, 0)),
                pl.BlockSpec(w_in.shape, lambda i: (0, 0), pipeline_mode=pl.Buffered(1)),
                full(qg.shape), full(kg.shape), full(bd.shape)]
    args = [x, mod3, w_in, qg, kg, bd]
    if rope:
        in_specs += [pl.BlockSpec((tt, LANES), lambda i: (i % nblk_seq, 0))] * 2
        args += list(rope_tabs)
    widths = [512, 512, 512, 512, 128, 128, 512]
    out_specs = [tok(w) for w in widths] + [tok(N_BRANCH * d)]
    out_shape = [jax.ShapeDtypeStruct((t, w), BF16) for w in widths] + [jax.ShapeDtypeStruct((t, N_BRANCH * d), F32)]
    if emit_kv:
        for w in (512, 512, 128, 128):
            out_specs.append(tok(w))
            out_shape.append(jax.ShapeDtypeStruct((t, w), F32))
    return pl.pallas_call(
        functools.partial(_in_proj_kernel, rope=rope, emit_kv=emit_kv),
        grid=(t // tt,),
        in_specs=in_specs,
        out_specs=out_specs,
        out_shape=out_shape,
        compiler_params=_cparams(("arbitrary",)),
        name="in_proj",
    )(*args)


def _attn_kernel(dq_ref, k_ref, v_ref, gq_ref, gk_ref, gv_ref, lam_ref, dng_ref, do_ref, go_ref, *, lam_init):
    tq = dq_ref.shape[1]
    lane = lax.broadcasted_iota(jnp.int32, (tq, LANES), 1)
    is_lo = lane < DIFF_HEAD_DIM
    m_lo = is_lo.astype(F32)
    m_hi = 1.0 - m_lo

    def split_halves(q):
        qf = q.astype(F32)
        return jnp.concatenate([(qf * m_lo).astype(BF16), (qf * m_hi).astype(BF16)], axis=0)

    lv = lam_ref[...]
    lam = (jnp.exp(jnp.sum(lv[0:1] * lv[1:2], keepdims=True))
           - jnp.exp(jnp.sum(lv[2:3] * lv[3:4], keepdims=True)) + lam_init)
    dng = dng_ref[...]

    for h in range(DIFF_HEADS):
        sl = slice(h * LANES, (h + 1) * LANES)
        s = _dot_nt(split_halves(dq_ref[0, :, sl]), k_ref[0, :, sl])
        e = jnp.exp2(s - jnp.max(s, -1, keepdims=True))
        a = e / jnp.sum(e, -1, keepdims=True)
        w = (a[:tq] - lam * a[tq:]).astype(BF16)
        o = _dot(w, v_ref[0, :, sl])
        o = o * lax.rsqrt(jnp.mean(o * o, -1, keepdims=True) + LN_EPS) * dng * (1.0 - lam_init)
        do_ref[0, :, sl] = o.astype(BF16)

    for hk in range(GQA_KV_HEADS):
        ksl = slice(hk * LANES, (hk + 1) * LANES)
        for m in range(2):
            c0 = (hk * 2 + m) * LANES
            sl = slice(c0, c0 + LANES)
            s = _dot_nt(split_halves(gq_ref[0, :, sl]), gk_ref[0, :, ksl])
            e = jnp.exp2(s - jnp.max(s, -1, keepdims=True))
            l = jnp.sum(e, -1, keepdims=True)
            o = _dot(e.astype(BF16), gv_ref[0, :, ksl]) / l
            go_ref[0, :, sl] = jnp.where(is_lo, o[:tq], o[tq:]).astype(BF16)


def _attn_call(dq, k, v, gq, gk, gv, lamv, dng, lam_init, tq):
    b, s, w = dq.shape
    sk = k.shape[1]
    qspec = pl.BlockSpec((1, tq, w), lambda bi, i: (bi, i, 0))
    kvspec = lambda width: pl.BlockSpec((1, sk, width), lambda bi, i: (bi, 0, 0))
    full = lambda shape: pl.BlockSpec(shape, lambda bi, i: (0,) * len(shape))
    return pl.pallas_call(
        functools.partial(_attn_kernel, lam_init=lam_init),
        grid=(b, s // tq),
        in_specs=[qspec, kvspec(512), kvspec(512), qspec, kvspec(256), kvspec(256),
                  full(lamv.shape), full(dng.shape)],
        out_specs=[qspec, qspec],
        out_shape=[jax.ShapeDtypeStruct((b, s, w), BF16)] * 2,
        compiler_params=_cparams(("arbitrary", "arbitrary")),
        name="attention",
    )(dq, k, v, gq, gk, gv, lamv, dng)


def _fourier_kernel(wc_ref, ws_ref, z_ref, cc_ref, sc_ref, o_ref, *, scale):
    z = z_ref[0]
    p = _dot(wc_ref[...], z).astype(BF16)
    q = _dot(ws_ref[...], z).astype(BF16)
    o_ref[0] = ((_dot(p, cc_ref[...]) - _dot(q, sc_ref[...])) * scale).astype(BF16)


def _fourier_call(z, wc, ws, cc, sc, tm):
    b, s, w = z.shape
    scale = 1.0 / math.sqrt(s * FOURIER_GROUP_DIM)
    return pl.pallas_call(
        functools.partial(_fourier_kernel, scale=scale),
        grid=(s // tm, b),
        in_specs=[pl.BlockSpec((tm, s), lambda i, bi: (i, 0)),
                  pl.BlockSpec((tm, s), lambda i, bi: (i, 0)),
                  pl.BlockSpec((1, s, w), lambda i, bi: (bi, 0, 0)),
                  pl.BlockSpec((w, w), lambda i, bi: (0, 0)),
                  pl.BlockSpec((w, w), lambda i, bi: (0, 0))],
        out_specs=pl.BlockSpec((1, tm, w), lambda i, bi: (bi, i, 0)),
        out_shape=jax.ShapeDtypeStruct((b, s, w), BF16),
        compiler_params=_cparams(("arbitrary", "arbitrary")),
        name="fourier",
    )(wc, ws, z, cc, sc)


def _angle_tables(rows, cols, period):
    m = (rows[:, None] * cols[None, :]) % period
    ang = m.astype(F32) * (2.0 * math.pi / period)
    return jnp.cos(ang), jnp.sin(ang)


def _dft_tables(n, split=64):
    j = jnp.arange(n, dtype=jnp.int32)
    if n <= 8 * split or n % split:
        return _angle_tables(j, j, n)
    n1 = n // split
    ca, sa = _angle_tables(j % n1, jnp.arange(n1, dtype=jnp.int32), n1)
    cb, sb = _angle_tables(j, jnp.arange(split, dtype=jnp.int32), n)
    cos = ca[:, :, None] * cb[:, None, :] - sa[:, :, None] * sb[:, None, :]
    sin = sa[:, :, None] * cb[:, None, :] + ca[:, :, None] * sb[:, None, :]
    return cos.reshape(n, n), sin.reshape(n, n)


def _merge_kernel(d_ref, g_ref, f_ref, gate_ref, x_ref, mod_ref, wb_ref, wo_ref, lng_ref, lnb_ref, o_ref):
    d = D_MODEL
    m = (gate_ref[:, 0:d] * _dot(d_ref[...], wb_ref[0])
         + gate_ref[:, d:2 * d] * _dot(g_ref[...], wb_ref[1])
         + gate_ref[:, 2 * d:3 * d] * _dot(f_ref[...], wb_ref[2]))
    mix = _dot(m.astype(BF16), wo_ref[...])
    g1 = mod_ref[0][:, 2 * d:3 * d]
    u = DEEPNORM_ALPHA * x_ref[...] + g1 * mix
    o_ref[...] = _ln(u) * lng_ref[...] + lnb_ref[...]


def _merge_call(d_o, g_o, f_o, gates, x, mod3, mod_idx, wb, wo, lng, lnb, tt):
    t, d = x.shape
    full = lambda shape: pl.BlockSpec(shape, lambda i: (0,) * len(shape))
    tok = lambda w: pl.BlockSpec((tt, w), lambda i: (i, 0))
    return pl.pallas_call(
        _merge_kernel,
        grid=(t // tt,),
        in_specs=[tok(512), tok(512), tok(512), tok(N_BRANCH * d), tok(d),
                  pl.BlockSpec((1, 1, mod3.shape[2]), lambda i: (mod_idx(i), 0, 0)),
                  full(wb.shape), full(wo.shape), full(lng.shape), full(lnb.shape)],
        out_specs=tok(d),
        out_shape=jax.ShapeDtypeStruct((t, d), F32),
        compiler_params=_cparams(("arbitrary",)),
        name="merge",
    )(d_o, g_o, f_o, gates, x, mod3, wb, wo, lng, lnb)


N_TOP = PEER_TOPK + 1
PAD_TOP = 24
N_CAND = PAD_TOP + 7 * 8 + (PAD_TOP - 8)


def _top_values(x, n):
    vals = []
    for _ in range(n):
        m = jnp.max(x, axis=0, keepdims=True)
        vals.append(m)
        x = jnp.where(x >= m, NEG_INF, x)
    return vals


def _route_head(s0, s1, av_ref, bv_ref, cand_ref):
    a = _top_values(s0, N_TOP)
    b = _top_values(s1, N_TOP)
    for k in range(N_TOP):
        av_ref[k:k + 1, :] = a[k]
        bv_ref[k:k + 1, :] = b[k]
    av = av_ref[...]
    bv = bv_ref[...]
    cand_ref[0:PAD_TOP, :] = a[0] + bv
    for k in range(1, 8):
        cand_ref[PAD_TOP + 8 * (k - 1):PAD_TOP + 8 * k, :] = a[k] + bv[0:8]
    cand_ref[PAD_TOP + 56:N_CAND, :] = av[8:PAD_TOP] + b[0]
    cand = cand_ref[...]
    c = _top_values(cand, N_TOP)
    tau = 0.5 * (c[PEER_TOPK - 1] + c[PEER_TOPK])
    z = jnp.sum(jnp.where(cand >= tau, jnp.exp(cand - c[0]), 0.0), axis=0, keepdims=True)
    w0 = jnp.exp(s0 - a[0]) / z
    w1 = jnp.exp(s1 - b[0])
    return -s0, s1 - tau, w0, w1


def _route_kernel(x_ref, mod_ref, wq_ref, sk_ref, hb_ref, ns0_ref, a1_ref, w0_ref, w1_ref,
                  av_ref, bv_ref, cand_ref):
    d = D_MODEL
    tt = x_ref.shape[0]
    mod = mod_ref[0]
    sh2 = mod[:, 3 * d:4 * d]
    sc2 = mod[:, 4 * d:5 * d]
    h = _ln(x_ref[...]) * (1.0 + sc2) + sh2
    hb = h.astype(BF16)
    ht = h.T.astype(BF16)
    for cb in range(tt // MM_COLS):
        hb_ref[cb] = ht[:, cb * MM_COLS:(cb + 1) * MM_COLS]
    q = _dot(hb, wq_ref[...])
    lane = lax.broadcasted_iota(jnp.int32, (tt, LANES), 1)
    m_lo = (lane < PEER_KEYS // 2).astype(F32)
    m_hi = 1.0 - m_lo
    av_ref[...] = jnp.full(av_ref.shape, NEG_INF, F32)
    bv_ref[...] = jnp.full(bv_ref.shape, NEG_INF, F32)
    for h in range(PEER_HEADS):
        qh = q[:, h * LANES:(h + 1) * LANES]
        sk = sk_ref[h]
        s0 = _dot_nt(sk, (qh * m_lo).astype(BF16))
        s1 = _dot_nt(sk, (qh * m_hi).astype(BF16))
        ns0, a1, w0, w1 = _route_head(s0, s1, av_ref, bv_ref, cand_ref)
        for tc in range(tt // LANES):
            cols = slice(tc * LANES, (tc + 1) * LANES)
            ns0_ref[h, tc] = ns0[:, cols]
            a1_ref[h, tc] = a1[:, cols]
            w0_ref[h, tc] = w0[:, cols]
            w1_ref[h, tc] = w1[:, cols]


def _route_call(x, mod3, mod_idx, wq, skc, tt):
    t, d = x.shape
    full = lambda shape: pl.BlockSpec(shape, lambda i: (0,) * len(shape))
    rspec = pl.BlockSpec((PEER_HEADS, tt // LANES, PEER_KEYS, LANES), lambda i: (0, i, 0, 0))
    rshape = jax.ShapeDtypeStruct((PEER_HEADS, t // LANES, PEER_KEYS, LANES), F32)
    return pl.pallas_call(
        _route_kernel,
        grid=(t // tt,),
        in_specs=[pl.BlockSpec((tt, d), lambda i: (i, 0)),
                  pl.BlockSpec((1, 1, mod3.shape[2]), lambda i: (mod_idx(i), 0, 0)),
                  full(wq.shape), full(skc.shape)],
        out_specs=[pl.BlockSpec((tt // MM_COLS, d, MM_COLS), lambda i: (i, 0, 0)), rspec, rspec, rspec, rspec],
        out_shape=[jax.ShapeDtypeStruct((t // MM_COLS, d, MM_COLS), BF16), rshape, rshape, rshape, rshape],
        scratch_shapes=[pltpu.VMEM((PAD_TOP, tt), F32), pltpu.VMEM((PAD_TOP, tt), F32),
                        pltpu.VMEM((N_CAND, tt), F32)],
        compiler_params=_cparams(("arbitrary",)),
        name="peer_route",
    )(x, mod3, wq, skc)


MM_COLS = 256
G_ROWS = 32
G_REUSE = 8
PEER_CHUNK = 1024


def _peer_kernel(hb_ref, ns0_ref, a1_ref, w0_ref, w1_ref, u_ref, vt_ref, x_ref, mod_ref, lng_ref, lnb_ref,
                 o_ref, acc_ref, a_ref, ga_ref, *, ni):
    c = pl.program_id(1)
    n_cb = hb_ref.shape[0]
    n_tc = ns0_ref.shape[1]

    @pl.when(c == 0)
    def _():
        acc_ref[...] = jnp.zeros_like(acc_ref)

    for cb in range(n_cb):
        a_blk = _dot(u_ref[...], hb_ref[cb])
        a_ref[2 * cb] = a_blk[:, :LANES]
        a_ref[2 * cb + 1] = a_blk[:, LANES:]

    i_base = pl.multiple_of(c * ni, ni)
    n_jq = PEER_KEYS // G_ROWS

    def gate_column(tc, carry):
        for ig in range(ni // G_REUSE):
            i0 = pl.multiple_of(i_base + ig * G_REUSE, G_REUSE)
            nrows = [ns0_ref[h, tc, pl.ds(i0, G_REUSE), :] for h in range(PEER_HEADS)]
            wrows = [w0_ref[h, tc, pl.ds(i0, G_REUSE), :] for h in range(PEER_HEADS)]
            bn = [[jnp.broadcast_to(nrows[h][r:r + 1], (G_ROWS, LANES)) for r in range(G_REUSE)]
                  for h in range(PEER_HEADS)]
            bw = [[jnp.broadcast_to(wrows[h][r:r + 1], (G_ROWS, LANES)) for r in range(G_REUSE)]
                  for h in range(PEER_HEADS)]
            for jq in range(n_jq):
                jrows = slice(jq * G_ROWS, (jq + 1) * G_ROWS)
                g = [None] * G_REUSE
                for h in range(PEER_HEADS):
                    a1q = a1_ref[h, tc, jrows, :]
                    w1q = w1_ref[h, tc, jrows, :]
                    for r in range(G_REUSE):
                        term = jnp.where(a1q >= bn[h][r], w1q * bw[h][r], 0.0)
                        g[r] = term if g[r] is None else g[r] + term
                for r in range(G_REUSE):
                    rows = slice((ig * G_REUSE + r) * PEER_KEYS + jq * G_ROWS,
                                 (ig * G_REUSE + r) * PEER_KEYS + (jq + 1) * G_ROWS)
                    a = a_ref[tc, rows, :]
                    act = 0.5 * a * (1.0 + lax.erf(a * SQRT_HALF))
                    ga_ref[tc, rows, :] = (g[r] * act).astype(BF16)
        return carry

    lax.fori_loop(0, n_tc, gate_column, 0)

    ga = jnp.concatenate([ga_ref[tc] for tc in range(n_tc)], axis=1)
    acc_ref[...] += _dot(vt_ref[...], ga)

    @pl.when(c == pl.num_programs(1) - 1)
    def _():
        d = D_MODEL
        g2 = mod_ref[0][:, 5 * d:6 * d]
        u = DEEPNORM_ALPHA * x_ref[...] + g2 * acc_ref[...].T
        o_ref[...] = _ln(u) * lng_ref[...] + lnb_ref[...]


def _peer_call(hb, ns0, a1, w0, w1, u_tab, vt_tab, x, mod3, mod_idx, lng, lnb, tt, ec):
    t, d = x.shape
    ne = u_tab.shape[0]
    ni = ec // PEER_KEYS
    full = lambda shape: pl.BlockSpec(shape, lambda i, c: (0,) * len(shape))
    rspec = pl.BlockSpec((PEER_HEADS, tt // LANES, PEER_KEYS, LANES), lambda i, c: (0, i, 0, 0))
    return pl.pallas_call(
        functools.partial(_peer_kernel, ni=ni),
        grid=(t // tt, ne // ec),
        in_specs=[pl.BlockSpec((tt // MM_COLS, d, MM_COLS), lambda i, c: (i, 0, 0)), rspec, rspec, rspec, rspec,
                  pl.BlockSpec((ec, d), lambda i, c: (c, 0)),
                  pl.BlockSpec((d, ec), lambda i, c: (0, c)),
                  pl.BlockSpec((tt, d), lambda i, c: (i, 0)),
                  pl.BlockSpec((1, 1, mod3.shape[2]), lambda i, c: (mod_idx(i), 0, 0)),
                  full(lng.shape), full(lnb.shape)],
        out_specs=pl.BlockSpec((tt, d), lambda i, c: (i, 0)),
        out_shape=jax.ShapeDtypeStruct((t, d), F32),
        scratch_shapes=[pltpu.VMEM((d, tt), F32), pltpu.VMEM((tt // LANES, ec, LANES), F32),
                        pltpu.VMEM((tt // LANES, ec, LANES), BF16)],
        compiler_params=_cparams(("arbitrary", "arbitrary")),
        name="peer_experts",
    )(hb, ns0, a1, w0, w1, u_tab, vt_tab, x, mod3, lng, lnb)


def _pick_tile(n, want):
    t = min(n, want)
    while n % t:
        t //= 2
    return t


def _rope_tables(seq):
    n_rows = seq // GRID_W
    row = jnp.repeat(jnp.arange(n_rows), GRID_W).astype(F32)
    col = jnp.tile(jnp.arange(GRID_W), n_rows).astype(F32)
    n_freq = DIFF_HEAD_DIM // 4
    freqs = ROPE_THETA ** (-jnp.arange(n_freq, dtype=F32) / n_freq)
    ang = jnp.concatenate([row[:, None] * freqs, col[:, None] * freqs], -1)
    cos = jnp.repeat(jnp.cos(ang), 2, axis=-1)
    sin = jnp.repeat(jnp.sin(ang), 2, axis=-1)
    sign = jnp.where(jnp.arange(DIFF_HEAD_DIM) % 2 == 0, -1.0, 1.0).astype(F32)
    reps = LANES // DIFF_HEAD_DIM
    return jnp.tile(cos, (1, reps)), jnp.tile(sin * sign, (1, reps))


def _dup_heads(a):
    b, s, _ = a.shape
    a4 = a.reshape(b, s, GQA_KV_HEADS, 1, GQA_HEAD_DIM)
    return jnp.broadcast_to(a4, (b, s, GQA_KV_HEADS, 2, GQA_HEAD_DIM)).reshape(b, s, 4 * GQA_HEAD_DIM)


def _layer(x, batch, seq, mod3, mod_idx_fn, lw, rope_tabs, ctx_kv, dft, emit_kv):
    t = batch * seq
    tile = lambda want: _pick_tile(seq if rope_tabs is not None else t, want)
    tt = tile(256)
    mod_idx = lambda i: mod_idx_fn(i, max(seq // tt, 1))
    outs = _in_proj_call(x, mod3, mod_idx, lw["w_in"], lw["qg"], lw["kg"], lw["bd"], rope_tabs, seq, tt, emit_kv)
    dq, dk, dv, gq, gk, gv, fz, gates = outs[:8]
    r3 = lambda a: a.reshape(batch, seq, a.shape[-1])
    k_att, v_att, gk_att, gv_att = r3(dk), r3(dv), r3(gk), r3(gv)
    if ctx_kv is not None:
        cdk, cdv, cgk, cgv = ctx_kv
        k_att = jnp.concatenate([k_att, cdk], axis=1)
        v_att = jnp.concatenate([v_att, cdv], axis=1)
        gk_att = jnp.concatenate([gk_att, cgk], axis=1)
        gv_att = jnp.concatenate([gv_att, cgv], axis=1)
    tq = _pick_tile(seq, 128 if k_att.shape[1] > 1024 else 256)
    d_o, g_o = _attn_call(r3(dq), k_att, v_att, r3(gq), _dup_heads(gk_att), _dup_heads(gv_att),
                          lw["diff_lam"], lw["dng"], lw["lam_init"], tq)
    f_o = _fourier_call(r3(fz), dft[0], dft[1], lw["cc"], lw["sc"], _pick_tile(seq, 512))
    flat = lambda a: a.reshape(t, a.shape[-1])
    tm = tile(512)
    mod_idx_m = lambda i: mod_idx_fn(i, max(seq // tm, 1))
    x1 = _merge_call(flat(d_o), flat(g_o), flat(f_o), gates, x, mod3, mod_idx_m, lw["wb"], lw["wo"],
                     lw["lng0"], lw["lnb0"], tm)
    tr = tile(256)
    mod_idx_r = lambda i: mod_idx_fn(i, max(seq // tr, 1))
    hb, ns0, a1, w0, w1 = _route_call(x1, mod3, mod_idx_r, lw["wq"], lw["skc"], tr)
    x2 = _peer_call(hb, ns0, a1, w0, w1, lw["u"], lw["vt"], x1, mod3, mod_idx_m, lw["lng1"], lw["lnb1"], tm, PEER_CHUNK)
    return x2, outs[8:]


def kernel(x_prompt, x_sample, cache_diff_k, cache_diff_v, cache_gqa_k, cache_gqa_v, c, c_ctx, w_mod, b_mod, w_in,
           diff_lam, diff_norm_g, q_norm_g, k_norm_g, w_branch, w_out, ln_g, ln_b, peer_wq, peer_subkeys, peer_u,
           peer_v):
    depth = w_in.shape[0]
    batch, seq, d = x_prompt.shape
    dbatch, dseq, _ = x_sample.shape
    past = cache_diff_k.shape[2]

    rows = ((dbatch + 1 + 7) // 8) * 8
    cs = jnp.zeros((rows, d), F32).at[:dbatch].set(c).at[dbatch].set(c_ctx)
    mod_all = _mod_call(cs, w_mod, b_mod)

    w_in_b = w_in.astype(BF16)
    wb_b = w_branch.astype(BF16)
    wo_b = w_out.astype(BF16)
    wq_b = peer_wq.astype(BF16)
    u_b = peer_u.astype(BF16)
    vt_b = jnp.swapaxes(peer_v, 1, 2).astype(BF16)
    skc = jnp.transpose(peer_subkeys, (0, 1, 3, 2, 4)).reshape(depth, PEER_HEADS, PEER_KEYS, PEER_KEYS).astype(BF16)
    bd = jnp.asarray(np.kron(np.eye(8, dtype=np.float32), np.full((64, 64), 1.0 / 64, np.float32)), BF16)
    ck, sk_ = _dft_tables(FOURIER_GROUP_DIM)
    eye_g = jnp.eye(FOURIER_GROUPS, dtype=F32)
    cc = jnp.kron(eye_g, ck).astype(BF16)
    sc = jnp.kron(eye_g, sk_).astype(BF16)
    dft_ctx = tuple(a.astype(BF16) for a in _dft_tables(seq))
    dft_lat = tuple(a.astype(BF16) for a in _dft_tables(dseq))
    rope_tabs = _rope_tables(dseq)

    def layer_weights(l):
        return {"w_in": w_in_b[l], "qg": jnp.tile(q_norm_g[l], 8)[None, :], "kg": jnp.tile(k_norm_g[l], 2)[None, :],
                "bd": bd, "diff_lam": diff_lam[l], "dng": diff_norm_g[l][None, :],
                "lam_init": 0.8 - 0.6 * math.exp(-0.3 * l), "cc": cc, "sc": sc, "wb": wb_b[l], "wo": wo_b[l],
                "lng0": ln_g[l, 0][None, :], "lnb0": ln_b[l, 0][None, :], "lng1": ln_g[l, 1][None, :],
                "lnb1": ln_b[l, 1][None, :], "wq": wq_b[l], "skc": skc[l], "u": u_b[l], "vt": vt_b[l]}

    xp = x_prompt.reshape(batch * seq, d)
    ctx_idx = lambda i, bps: dbatch
    kv = [[], [], [], []]
    for l in range(depth):
        mod3 = mod_all[l].reshape(rows, 1, 6 * d)
        xp, own = _layer(xp, batch, seq, mod3, ctx_idx, layer_weights(l), None, None, dft_ctx, True)
        for lst, a in zip(kv, own):
            lst.append(a)
    new_diff_k = jnp.stack(kv[0], 0).reshape(depth, batch, seq, DIFF_HEADS, 2 * DIFF_HEAD_DIM).swapaxes(0, 1)
    new_diff_v = jnp.stack(kv[1], 0).reshape(depth, batch, seq, DIFF_HEADS, 2 * DIFF_HEAD_DIM).swapaxes(0, 1)
    new_gqa_k = jnp.stack(kv[2], 0).reshape(depth, batch, seq, GQA_KV_HEADS, GQA_HEAD_DIM).swapaxes(0, 1)
    new_gqa_v = jnp.stack(kv[3], 0).reshape(depth, batch, seq, GQA_KV_HEADS, GQA_HEAD_DIM).swapaxes(0, 1)

    xs = x_sample.reshape(dbatch * dseq, d)
    lat_idx = lambda i, bps: i // bps
    for l in range(depth):
        mod3 = mod_all[l].reshape(rows, 1, 6 * d)
        ctx_kv = (cache_diff_k[:, l].reshape(dbatch, past, 512).astype(BF16),
                  cache_diff_v[:, l].reshape(dbatch, past, 512).astype(BF16),
                  cache_gqa_k[:, l].reshape(dbatch, past, 128).astype(BF16),
                  cache_gqa_v[:, l].reshape(dbatch, past, 128).astype(BF16))
        xs, _ = _layer(xs, dbatch, dseq, mod3, lat_idx, layer_weights(l), rope_tabs, ctx_kv, dft_lat, False)

    return (xp.reshape(batch, seq, d), xs.reshape(dbatch, dseq, d), new_diff_k, new_diff_v, new_gqa_k, new_gqa_v)
```

```python
import functools
import math

import numpy as np
import jax
import jax.numpy as jnp
from jax import lax
from jax.experimental import pallas as pl
from jax.experimental.pallas import tpu as pltpu

F32 = jnp.float32
BF16 = jnp.bfloat16

D_MODEL = 1024
GRID_W = 64
ROPE_THETA = 10000.0
DIFF_HEADS = 4
DIFF_HEAD_DIM = 64
GQA_KV_HEADS = 2
GQA_HEAD_DIM = 64
FOURIER_GROUP_DIM = 128
FOURIER_GROUPS = 4
N_BRANCH = 3
BRANCH_WIDTH = 512
C_DQ, C_DK, C_DV, C_GQ, C_GK, C_GV, C_FZ, C_GZ, C_END = 0, 512, 1024, 1536, 2048, 2176, 2304, 2816, 5888
PEER_HEADS = 8
PEER_KEYS = 128
PEER_EXPERTS = PEER_KEYS * PEER_KEYS
PEER_TOPK = 16
DEPTH_FOR_NORM = 4
DEEPNORM_ALPHA = (2 * DEPTH_FOR_NORM) ** 0.25
LN_EPS = 1e-6
ATTN_SCALE = 0.125 * math.log2(math.e)
SQRT_HALF = math.sqrt(0.5)

LANES = 128
VMEM_LIMIT = 56 * 1024 * 1024
NEG_INF = float("-inf")


def _cparams(sem):
    return pltpu.CompilerParams(dimension_semantics=sem, vmem_limit_bytes=VMEM_LIMIT)


def _ln(x):
    mu = jnp.mean(x, -1, keepdims=True)
    xc = x - mu
    return xc * lax.rsqrt(jnp.mean(xc * xc, -1, keepdims=True) + LN_EPS)


def _dot(a, b):
    return jnp.dot(a, b, preferred_element_type=F32)


def _dot_nt(a, b):
    return lax.dot_general(a, b, (((1,), (1,)), ((), ())), preferred_element_type=F32)


def _mod_kernel(c_ref, w_ref, b_ref, o_ref):
    c = c_ref[...]
    a = c * jax.nn.sigmoid(c)
    o_ref[0] = jnp.dot(a, w_ref[0], preferred_element_type=F32, precision=lax.Precision.HIGHEST) + b_ref[0]


def _mod_call(cs, w_mod, b_mod):
    depth, d, n = w_mod.shape
    rows = cs.shape[0]
    nt = 1536
    return pl.pallas_call(
        _mod_kernel,
        grid=(depth, n // nt),
        in_specs=[pl.BlockSpec((rows, d), lambda l, j: (0, 0)),
                  pl.BlockSpec((1, d, nt), lambda l, j: (l, 0, j)),
                  pl.BlockSpec((1, 1, nt), lambda l, j: (l, 0, j))],
        out_specs=pl.BlockSpec((1, rows, nt), lambda l, j: (l, 0, j)),
        out_shape=jax.ShapeDtypeStruct((depth, rows, n), F32),
        compiler_params=_cparams(("arbitrary", "arbitrary")),
        name="mod_proj",
    )(cs, w_mod, b_mod.reshape(depth, 1, n))


def _rms64(y, bd, g):
    y2 = y * y
    hi = y2.astype(BF16)
    lo = (y2 - hi.astype(F32)).astype(BF16)
    ms = _dot(hi, bd) + _dot(lo, bd)
    return y * lax.rsqrt(ms + LN_EPS) * g


def _rope(y, cos, sin_signed):
    w = y.shape[1]
    reps = w // LANES
    c = jnp.concatenate([cos] * reps, axis=1) if reps > 1 else cos
    s = jnp.concatenate([sin_signed] * reps, axis=1) if reps > 1 else sin_signed
    lane = lax.broadcasted_iota(jnp.int32, y.shape, 1)
    even = (lane & 1) == 0
    partner = jnp.where(even, pltpu.roll(y, w - 1, 1), pltpu.roll(y, 1, 1))
    return y * c + partner * s


def _in_proj_kernel(*refs, rope, emit_kv):
    x_ref, mod_ref, w_ref, qg_ref, kg_ref, bd_ref = refs[:6]
    pos = 6
    if rope:
        cos_ref, sin_ref = refs[6:8]
        pos = 8
    dq_ref, dk_ref, dv_ref, gq_ref, gk_ref, gv_ref, fz_ref, gate_ref = refs[pos:pos + 8]
    pos += 8
    if emit_kv:
        ndk_ref, ndv_ref, ngk_ref, ngv_ref = refs[pos:pos + 4]

    mod = mod_ref[0]
    sh1 = mod[:, 0:D_MODEL]
    sc1 = mod[:, D_MODEL:2 * D_MODEL]
    h = (_ln(x_ref[...]) * (1.0 + sc1) + sh1).astype(BF16)

    def seg(lo, hi):
        return _dot(h, w_ref[:, lo:hi])

    if rope:
        cos = cos_ref[...]
        sin = sin_ref[...]
        rot = lambda y: _rope(y, cos, sin)
    else:
        rot = lambda y: y

    dq_ref[...] = (rot(seg(C_DQ, C_DK)) * ATTN_SCALE).astype(BF16)
    dk = seg(C_DK, C_DV)
    dk_ref[...] = rot(dk).astype(BF16)
    dv = seg(C_DV, C_GQ)
    dv_ref[...] = dv.astype(BF16)
    gq = _rms64(seg(C_GQ, C_GK), bd_ref[...], qg_ref[...])
    gq_ref[...] = (rot(gq) * ATTN_SCALE).astype(BF16)
    gk = _rms64(seg(C_GK, C_GV), bd_ref[0:LANES, 0:LANES], kg_ref[...])
    gk_ref[...] = rot(gk).astype(BF16)
    gv = seg(C_GV, C_FZ)
    gv_ref[...] = gv.astype(BF16)
    fz_ref[...] = seg(C_FZ, C_GZ).astype(BF16)
    for n in range(N_BRANCH):
        lo = C_GZ + n * D_MODEL
        gate_ref[:, n * D_MODEL:(n + 1) * D_MODEL] = jax.nn.sigmoid(seg(lo, lo + D_MODEL))
    if emit_kv:
        ndk_ref[...] = dk
        ndv_ref[...] = dv
        ngk_ref[...] = gk
        ngv_ref[...] = gv


def _in_proj_call(x, mod3, mod_idx, w_in, qg, kg, bd, rope_tabs, seq, tt, emit_kv):
    t, d = x.shape
    rope = rope_tabs is not None
    nblk_seq = seq // tt
    full = lambda shape: pl.BlockSpec(shape, lambda i: (0,) * len(shape))
    tok = lambda w: pl.BlockSpec((tt, w), lambda i: (i, 0))
    in_specs = [tok(d),
                pl.BlockSpec((1, 1, mod3.shape[2]), lambda i: (mod_idx(i), 0, 0)),
                pl.BlockSpec(w_in.shape, lambda i: (0, 0), pipeline_mode=pl.Buffered(1)),
                full(qg.shape), full(kg.shape), full(bd.shape)]
    args = [x, mod3, w_in, qg, kg, bd]
    if rope:
        in_specs += [pl.BlockSpec((tt, LANES), lambda i: (i % nblk_seq, 0))] * 2
        args += list(rope_tabs)
    widths = [512, 512, 512, 512, 128, 128, 512]
    out_specs = [tok(w) for w in widths] + [tok(N_BRANCH * d)]
    out_shape = [jax.ShapeDtypeStruct((t, w), BF16) for w in widths] + [jax.ShapeDtypeStruct((t, N_BRANCH * d), F32)]
    if emit_kv:
        for w in (512, 512, 128, 128):
            out_specs.append(tok(w))
            out_shape.append(jax.ShapeDtypeStruct((t, w), F32))
    return pl.pallas_call(
        functools.partial(_in_proj_kernel, rope=rope, emit_kv=emit_kv),
        grid=(t // tt,),
        in_specs=in_specs,
        out_specs=out_specs,
        out_shape=out_shape,
        compiler_params=_cparams(("arbitrary",)),
        name="in_proj",
    )(*args)


def _attn_kernel(dq_ref, k_ref, v_ref, gq_ref, gk_ref, gv_ref, lam_ref, dng_ref, do_ref, go_ref, *, lam_init):
    tq = dq_ref.shape[1]
    lane = lax.broadcasted_iota(jnp.int32, (tq, LANES), 1)
    is_lo = lane < DIFF_HEAD_DIM
    m_lo = is_lo.astype(F32)
    m_hi = 1.0 - m_lo

    def split_halves(q):
        qf = q.astype(F32)
        return jnp.concatenate([(qf * m_lo).astype(BF16), (qf * m_hi).astype(BF16)], axis=0)

    lv = lam_ref[...]
    lam = (jnp.exp(jnp.sum(lv[0:1] * lv[1:2], keepdims=True))
           - jnp.exp(jnp.sum(lv[2:3] * lv[3:4], keepdims=True)) + lam_init)
    dng = dng_ref[...]

    for h in range(DIFF_HEADS):
        sl = slice(h * LANES, (h + 1) * LANES)
        s = _dot_nt(split_halves(dq_ref[0, :, sl]), k_ref[0, :, sl])
        e = jnp.exp2(s - jnp.max(s, -1, keepdims=True))
        r = 1.0 / jnp.sum(e, -1, keepdims=True)
        w = (e[:tq] * r[:tq] - e[tq:] * (lam * r[tq:])).astype(BF16)
        o = _dot(w, v_ref[0, :, sl])
        o = o * lax.rsqrt(jnp.mean(o * o, -1, keepdims=True) + LN_EPS) * dng * (1.0 - lam_init)
        do_ref[0, :, sl] = o.astype(BF16)

    for hk in range(GQA_KV_HEADS):
        ksl = slice(hk * LANES, (hk + 1) * LANES)
        for m in range(2):
            c0 = (hk * 2 + m) * LANES
            sl = slice(c0, c0 + LANES)
            s = _dot_nt(split_halves(gq_ref[0, :, sl]), gk_ref[0, :, ksl])
            e = jnp.exp2(s - jnp.max(s, -1, keepdims=True))
            l = jnp.sum(e, -1, keepdims=True)
            o = _dot(e.astype(BF16), gv_ref[0, :, ksl]) / l
            go_ref[0, :, sl] = jnp.where(is_lo, o[:tq], o[tq:]).astype(BF16)


def _attn_call(dq, k, v, gq, gk, gv, lamv, dng, lam_init, tq):
    b, s, w = dq.shape
    sk = k.shape[1]
    qspec = pl.BlockSpec((1, tq, w), lambda bi, i: (bi, i, 0))
    kvspec = lambda width: pl.BlockSpec((1, sk, width), lambda bi, i: (bi, 0, 0))
    full = lambda shape: pl.BlockSpec(shape, lambda bi, i: (0,) * len(shape))
    return pl.pallas_call(
        functools.partial(_attn_kernel, lam_init=lam_init),
        grid=(b, s // tq),
        in_specs=[qspec, kvspec(512), kvspec(512), qspec, kvspec(256), kvspec(256),
                  full(lamv.shape), full(dng.shape)],
        out_specs=[qspec, qspec],
        out_shape=[jax.ShapeDtypeStruct((b, s, w), BF16)] * 2,
        compiler_params=_cparams(("arbitrary", "arbitrary")),
        name="attention",
    )(dq, k, v, gq, gk, gv, lamv, dng)


def _fourier_kernel(wc_ref, ws_ref, z_ref, cc_ref, sc_ref, o_ref, *, scale):
    z = z_ref[0]
    p = _dot(wc_ref[...], z).astype(BF16)
    q = _dot(ws_ref[...], z).astype(BF16)
    o_ref[0] = ((_dot(p, cc_ref[...]) - _dot(q, sc_ref[...])) * scale).astype(BF16)


def _fourier_call(z, wc, ws, cc, sc, tm):
    b, s, w = z.shape
    scale = 1.0 / math.sqrt(s * FOURIER_GROUP_DIM)
    return pl.pallas_call(
        functools.partial(_fourier_kernel, scale=scale),
        grid=(s // tm, b),
        in_specs=[pl.BlockSpec((tm, s), lambda i, bi: (i, 0)),
                  pl.BlockSpec((tm, s), lambda i, bi: (i, 0)),
                  pl.BlockSpec((1, s, w), lambda i, bi: (bi, 0, 0)),
                  pl.BlockSpec((w, w), lambda i, bi: (0, 0)),
                  pl.BlockSpec((w, w), lambda i, bi: (0, 0))],
        out_specs=pl.BlockSpec((1, tm, w), lambda i, bi: (bi, i, 0)),
        out_shape=jax.ShapeDtypeStruct((b, s, w), BF16),
        compiler_params=_cparams(("arbitrary", "arbitrary")),
        name="fourier",
    )(wc, ws, z, cc, sc)


def _angle_tables(rows, cols, period):
    m = (rows[:, None] * cols[None, :]) % period
    ang = m.astype(F32) * (2.0 * math.pi / period)
    return jnp.cos(ang), jnp.sin(ang)


def _dft_tables(n, split=64):
    j = jnp.arange(n, dtype=jnp.int32)
    if n <= 8 * split or n % split:
        return _angle_tables(j, j, n)
    n1 = n // split
    ca, sa = _angle_tables(j % n1, jnp.arange(n1, dtype=jnp.int32), n1)
    cb, sb = _angle_tables(j, jnp.arange(split, dtype=jnp.int32), n)
    cos = ca[:, :, None] * cb[:, None, :] - sa[:, :, None] * sb[:, None, :]
    sin = sa[:, :, None] * cb[:, None, :] + ca[:, :, None] * sb[:, None, :]
    return cos.reshape(n, n), sin.reshape(n, n)


def _merge_kernel(d_ref, g_ref, f_ref, gate_ref, x_ref, mod_ref, wb_ref, wo_ref, lng_ref, lnb_ref, o_ref):
    d = D_MODEL
    m = (gate_ref[:, 0:d] * _dot(d_ref[...], wb_ref[0])
         + gate_ref[:, d:2 * d] * _dot(g_ref[...], wb_ref[1])
         + gate_ref[:, 2 * d:3 * d] * _dot(f_ref[...], wb_ref[2]))
    mix = _dot(m.astype(BF16), wo_ref[...])
    g1 = mod_ref[0][:, 2 * d:3 * d]
    u = DEEPNORM_ALPHA * x_ref[...] + g1 * mix
    o_ref[...] = _ln(u) * lng_ref[...] + lnb_ref[...]


def _merge_call(d_o, g_o, f_o, gates, x, mod3, mod_idx, wb, wo, lng, lnb, tt):
    t, d = x.shape
    full = lambda shape: pl.BlockSpec(shape, lambda i: (0,) * len(shape))
    tok = lambda w: pl.BlockSpec((tt, w), lambda i: (i, 0))
    return pl.pallas_call(
        _merge_kernel,
        grid=(t // tt,),
        in_specs=[tok(512), tok(512), tok(512), tok(N_BRANCH * d), tok(d),
                  pl.BlockSpec((1, 1, mod3.shape[2]), lambda i: (mod_idx(i), 0, 0)),
                  full(wb.shape), full(wo.shape), full(lng.shape), full(lnb.shape)],
        out_specs=tok(d),
        out_shape=jax.ShapeDtypeStruct((t, d), F32),
        compiler_params=_cparams(("arbitrary",)),
        name="merge",
    )(d_o, g_o, f_o, gates, x, mod3, wb, wo, lng, lnb)


N_TOP = PEER_TOPK + 1
PAD_TOP = 24
N_CAND = PAD_TOP + 7 * 8 + (PAD_TOP - 8)


def _top_values(x, n):
    vals = []
    for _ in range(n):
        m = jnp.max(x, axis=0, keepdims=True)
        vals.append(m)
        x = jnp.where(x >= m, NEG_INF, x)
    return vals


def _route_head(s0, s1, av_ref, bv_ref, cand_ref):
    a = _top_values(s0, N_TOP)
    b = _top_values(s1, N_TOP)
    for k in range(N_TOP):
        av_ref[k:k + 1, :] = a[k]
        bv_ref[k:k + 1, :] = b[k]
    av = av_ref[...]
    bv = bv_ref[...]
    cand_ref[0:PAD_TOP, :] = a[0] + bv
    for k in range(1, 8):
        cand_ref[PAD_TOP + 8 * (k - 1):PAD_TOP + 8 * k, :] = a[k] + bv[0:8]
    cand_ref[PAD_TOP + 56:N_CAND, :] = av[8:PAD_TOP] + b[0]
    cand = cand_ref[...]
    c = _top_values(cand, N_TOP)
    tau = 0.5 * (c[PEER_TOPK - 1] + c[PEER_TOPK])
    z = jnp.sum(jnp.where(cand >= tau, jnp.exp(cand - c[0]), 0.0), axis=0, keepdims=True)
    w0 = jnp.exp(s0 - a[0]) / z
    w1 = jnp.exp(s1 - b[0])
    return -s0, s1 - tau, w0, w1


def _route_kernel(x_ref, mod_ref, wq_ref, sk_ref, hb_ref, ns0_ref, a1_ref, w0_ref, w1_ref,
                  av_ref, bv_ref, cand_ref):
    d = D_MODEL
    tt = x_ref.shape[0]
    mod = mod_ref[0]
    sh2 = mod[:, 3 * d:4 * d]
    sc2 = mod[:, 4 * d:5 * d]
    h = _ln(x_ref[...]) * (1.0 + sc2) + sh2
    hb = h.astype(BF16)
    ht = h.T.astype(BF16)
    for cb in range(tt // MM_COLS):
        hb_ref[cb] = ht[:, cb * MM_COLS:(cb + 1) * MM_COLS]
    q = _dot(hb, wq_ref[...])
    lane = lax.broadcasted_iota(jnp.int32, (tt, LANES), 1)
    m_lo = (lane < PEER_KEYS // 2).astype(F32)
    m_hi = 1.0 - m_lo
    av_ref[...] = jnp.full(av_ref.shape, NEG_INF, F32)
    bv_ref[...] = jnp.full(bv_ref.shape, NEG_INF, F32)
    for h in range(PEER_HEADS):
        qh = q[:, h * LANES:(h + 1) * LANES]
        sk = sk_ref[h]
        s0 = _dot_nt(sk, (qh * m_lo).astype(BF16))
        s1 = _dot_nt(sk, (qh * m_hi).astype(BF16))
        ns0, a1, w0, w1 = _route_head(s0, s1, av_ref, bv_ref, cand_ref)
        for tc in range(tt // LANES):
            cols = slice(tc * LANES, (tc + 1) * LANES)
            ns0_ref[h, tc] = ns0[:, cols]
            a1_ref[h, tc] = a1[:, cols]
            w0_ref[h, tc] = w0[:, cols]
            w1_ref[h, tc] = w1[:, cols]


def _route_call(x, mod3, mod_idx, wq, skc, tt):
    t, d = x.shape
    full = lambda shape: pl.BlockSpec(shape, lambda i: (0,) * len(shape))
    rspec = pl.BlockSpec((PEER_HEADS, tt // LANES, PEER_KEYS, LANES), lambda i: (0, i, 0, 0))
    rshape = jax.ShapeDtypeStruct((PEER_HEADS, t // LANES, PEER_KEYS, LANES), F32)
    return pl.pallas_call(
        _route_kernel,
        grid=(t // tt,),
        in_specs=[pl.BlockSpec((tt, d), lambda i: (i, 0)),
                  pl.BlockSpec((1, 1, mod3.shape[2]), lambda i: (mod_idx(i), 0, 0)),
                  full(wq.shape), full(skc.shape)],
        out_specs=[pl.BlockSpec((tt // MM_COLS, d, MM_COLS), lambda i: (i, 0, 0)), rspec, rspec, rspec, rspec],
        out_shape=[jax.ShapeDtypeStruct((t // MM_COLS, d, MM_COLS), BF16), rshape, rshape, rshape, rshape],
        scratch_shapes=[pltpu.VMEM((PAD_TOP, tt), F32), pltpu.VMEM((PAD_TOP, tt), F32),
                        pltpu.VMEM((N_CAND, tt), F32)],
        compiler_params=_cparams(("arbitrary",)),
        name="peer_route",
    )(x, mod3, wq, skc)


MM_COLS = 256
G_ROWS = 32
G_REUSE = 8
PEER_CHUNK = 1024


def _peer_kernel(hb_ref, ns0_ref, a1_ref, w0_ref, w1_ref, u_ref, vt_ref, x_ref, mod_ref, lng_ref, lnb_ref,
                 o_ref, acc_ref, a_ref, ga_ref, *, ni):
    c = pl.program_id(1)
    n_cb = hb_ref.shape[0]
    n_tc = ns0_ref.shape[1]

    @pl.when(c == 0)
    def _():
        acc_ref[...] = jnp.zeros_like(acc_ref)

    for cb in range(n_cb):
        a_blk = _dot(u_ref[...], hb_ref[cb])
        a_ref[2 * cb] = a_blk[:, :LANES]
        a_ref[2 * cb + 1] = a_blk[:, LANES:]

    i_base = pl.multiple_of(c * ni, ni)
    n_jq = PEER_KEYS // G_ROWS

    def gate_column(tc, carry):
        for ig in range(ni // G_REUSE):
            i0 = pl.multiple_of(i_base + ig * G_REUSE, G_REUSE)
            nrows = [ns0_ref[h, tc, pl.ds(i0, G_REUSE), :] for h in range(PEER_HEADS)]
            wrows = [w0_ref[h, tc, pl.ds(i0, G_REUSE), :] for h in range(PEER_HEADS)]
            bn = [[jnp.broadcast_to(nrows[h][r:r + 1], (G_ROWS, LANES)) for r in range(G_REUSE)]
                  for h in range(PEER_HEADS)]
            bw = [[jnp.broadcast_to(wrows[h][r:r + 1], (G_ROWS, LANES)) for r in range(G_REUSE)]
                  for h in range(PEER_HEADS)]
            for jq in range(n_jq):
                jrows = slice(jq * G_ROWS, (jq + 1) * G_ROWS)
                g = [None] * G_REUSE
                for h in range(PEER_HEADS):
                    a1q = a1_ref[h, tc, jrows, :]
                    w1q = w1_ref[h, tc, jrows, :]
                    for r in range(G_REUSE):
                        term = jnp.where(a1q >= bn[h][r], w1q * bw[h][r], 0.0)
                        g[r] = term if g[r] is None else g[r] + term
                for r in range(G_REUSE):
                    rows = slice((ig * G_REUSE + r) * PEER_KEYS + jq * G_ROWS,
                                 (ig * G_REUSE + r) * PEER_KEYS + (jq + 1) * G_ROWS)
                    a = a_ref[tc, rows, :]
                    act = 0.5 * a * (1.0 + lax.erf(a * SQRT_HALF))
                    ga_ref[tc, rows, :] = (g[r] * act).astype(BF16)
        return carry

    lax.fori_loop(0, n_tc, gate_column, 0)

    ga = jnp.concatenate([ga_ref[tc] for tc in range(n_tc)], axis=1)
    acc_ref[...] += _dot(vt_ref[...], ga)

    @pl.when(c == pl.num_programs(1) - 1)
    def _():
        d = D_MODEL
        g2 = mod_ref[0][:, 5 * d:6 * d]
        u = DEEPNORM_ALPHA * x_ref[...] + g2 * acc_ref[...].T
        o_ref[...] = _ln(u) * lng_ref[...] + lnb_ref[...]


def _peer_call(hb, ns0, a1, w0, w1, u_tab, vt_tab, x, mod3, mod_idx, lng, lnb, tt, ec):
    t, d = x.shape
    ne = u_tab.shape[0]
    ni = ec // PEER_KEYS
    full = lambda shape: pl.BlockSpec(shape, lambda i, c: (0,) * len(shape))
    rspec = pl.BlockSpec((PEER_HEADS, tt // LANES, PEER_KEYS, LANES), lambda i, c: (0, i, 0, 0))
    return pl.pallas_call(
        functools.partial(_peer_kernel, ni=ni),
        grid=(t // tt, ne // ec),
        in_specs=[pl.BlockSpec((tt // MM_COLS, d, MM_COLS), lambda i, c: (i, 0, 0)), rspec, rspec, rspec, rspec,
                  pl.BlockSpec((ec, d), lambda i, c: (c, 0)),
                  pl.BlockSpec((d, ec), lambda i, c: (0, c)),
                  pl.BlockSpec((tt, d), lambda i, c: (i, 0)),
                  pl.BlockSpec((1, 1, mod3.shape[2]), lambda i, c: (mod_idx(i), 0, 0)),
                  full(lng.shape), full(lnb.shape)],
        out_specs=pl.BlockSpec((tt, d), lambda i, c: (i, 0)),
        out_shape=jax.ShapeDtypeStruct((t, d), F32),
        scratch_shapes=[pltpu.VMEM((d, tt), F32), pltpu.VMEM((tt // LANES, ec, LANES), F32),
                        pltpu.VMEM((tt // LANES, ec, LANES), BF16)],
        compiler_params=_cparams(("arbitrary", "arbitrary")),
        name="peer_experts",
    )(hb, ns0, a1, w0, w1, u_tab, vt_tab, x, mod3, lng, lnb)


TILE_IN_PROJ = 256
TILE_TOKENS = 512
TILE_ROUTE = 256
TILE_FOURIER = 512
TILE_Q_LONG = 128
TILE_Q_SHORT = 256
LONG_KEYS = 1024


def _pick_tile(n, want):
    t = min(n, want)
    while n % t:
        t //= 2
    return t


def _rope_tables(seq):
    n_rows = seq // GRID_W
    row = jnp.repeat(jnp.arange(n_rows), GRID_W).astype(F32)
    col = jnp.tile(jnp.arange(GRID_W), n_rows).astype(F32)
    n_freq = DIFF_HEAD_DIM // 4
    freqs = ROPE_THETA ** (-jnp.arange(n_freq, dtype=F32) / n_freq)
    ang = jnp.concatenate([row[:, None] * freqs, col[:, None] * freqs], -1)
    cos = jnp.repeat(jnp.cos(ang), 2, axis=-1)
    sin = jnp.repeat(jnp.sin(ang), 2, axis=-1)
    sign = jnp.where(jnp.arange(DIFF_HEAD_DIM) % 2 == 0, -1.0, 1.0).astype(F32)
    reps = LANES // DIFF_HEAD_DIM
    return jnp.tile(cos, (1, reps)), jnp.tile(sin * sign, (1, reps))


def _dup_heads(a):
    b, s, _ = a.shape
    a4 = a.reshape(b, s, GQA_KV_HEADS, 1, GQA_HEAD_DIM)
    return jnp.broadcast_to(a4, (b, s, GQA_KV_HEADS, 2, GQA_HEAD_DIM)).reshape(b, s, 4 * GQA_HEAD_DIM)


def _layer(x, batch, seq, mod3, mod_idx_fn, lw, rope_tabs, ctx_kv, dft, emit_kv):
    t = batch * seq
    tile = lambda want: _pick_tile(seq if rope_tabs is not None else t, want)
    tt = tile(TILE_IN_PROJ)
    mod_idx = lambda i: mod_idx_fn(i, max(seq // tt, 1))
    outs = _in_proj_call(x, mod3, mod_idx, lw["w_in"], lw["qg"], lw["kg"], lw["bd"], rope_tabs, seq, tt, emit_kv)
    dq, dk, dv, gq, gk, gv, fz, gates = outs[:8]
    r3 = lambda a: a.reshape(batch, seq, a.shape[-1])
    k_att, v_att, gk_att, gv_att = r3(dk), r3(dv), r3(gk), r3(gv)
    if ctx_kv is not None:
        cdk, cdv, cgk, cgv = ctx_kv
        k_att = jnp.concatenate([k_att, cdk], axis=1)
        v_att = jnp.concatenate([v_att, cdv], axis=1)
        gk_att = jnp.concatenate([gk_att, cgk], axis=1)
        gv_att = jnp.concatenate([gv_att, cgv], axis=1)
    tq = _pick_tile(seq, TILE_Q_LONG if k_att.shape[1] > LONG_KEYS else TILE_Q_SHORT)
    d_o, g_o = _attn_call(r3(dq), k_att, v_att, r3(gq), _dup_heads(gk_att), _dup_heads(gv_att),
                          lw["diff_lam"], lw["dng"], lw["lam_init"], tq)
    f_o = _fourier_call(r3(fz), dft[0], dft[1], lw["cc"], lw["sc"], _pick_tile(seq, TILE_FOURIER))
    flat = lambda a: a.reshape(t, a.shape[-1])
    tm = tile(TILE_TOKENS)
    mod_idx_m = lambda i: mod_idx_fn(i, max(seq // tm, 1))
    x1 = _merge_call(flat(d_o), flat(g_o), flat(f_o), gates, x, mod3, mod_idx_m, lw["wb"], lw["wo"],
                     lw["lng0"], lw["lnb0"], tm)
    tr = tile(TILE_ROUTE)
    mod_idx_r = lambda i: mod_idx_fn(i, max(seq // tr, 1))
    hb, ns0, a1, w0, w1 = _route_call(x1, mod3, mod_idx_r, lw["wq"], lw["skc"], tr)
    x2 = _peer_call(hb, ns0, a1, w0, w1, lw["u"], lw["vt"], x1, mod3, mod_idx_m, lw["lng1"], lw["lnb1"], tm, PEER_CHUNK)
    return x2, outs[8:]


def kernel(x_prompt, x_sample, cache_diff_k, cache_diff_v, cache_gqa_k, cache_gqa_v, c, c_ctx, w_mod, b_mod, w_in,
           diff_lam, diff_norm_g, q_norm_g, k_norm_g, w_branch, w_out, ln_g, ln_b, peer_wq, peer_subkeys, peer_u,
           peer_v):
    depth = w_in.shape[0]
    batch, seq, d = x_prompt.shape
    dbatch, dseq, _ = x_sample.shape
    past = cache_diff_k.shape[2]

    rows = ((dbatch + 1 + 7) // 8) * 8
    cs = jnp.zeros((rows, d), F32).at[:dbatch].set(c).at[dbatch].set(c_ctx)
    mod_all = _mod_call(cs, w_mod, b_mod)

    w_in_b = w_in.astype(BF16)
    wb_b = w_branch.astype(BF16)
    wo_b = w_out.astype(BF16)
    wq_b = peer_wq.astype(BF16)
    u_b = peer_u.astype(BF16)
    vt_b = jnp.swapaxes(peer_v, 1, 2).astype(BF16)
    skc = jnp.transpose(peer_subkeys, (0, 1, 3, 2, 4)).reshape(depth, PEER_HEADS, PEER_KEYS, PEER_KEYS).astype(BF16)
    bd = jnp.asarray(np.kron(np.eye(8, dtype=np.float32), np.full((64, 64), 1.0 / 64, np.float32)), BF16)
    ck, sk_ = _dft_tables(FOURIER_GROUP_DIM)
    eye_g = jnp.eye(FOURIER_GROUPS, dtype=F32)
    cc = jnp.kron(eye_g, ck).astype(BF16)
    sc = jnp.kron(eye_g, sk_).astype(BF16)
    dft_ctx = tuple(a.astype(BF16) for a in _dft_tables(seq))
    dft_lat = tuple(a.astype(BF16) for a in _dft_tables(dseq))
    rope_tabs = _rope_tables(dseq)

    def layer_weights(l):
        return {"w_in": w_in_b[l], "qg": jnp.tile(q_norm_g[l], 8)[None, :], "kg": jnp.tile(k_norm_g[l], 2)[None, :],
                "bd": bd, "diff_lam": diff_lam[l], "dng": diff_norm_g[l][None, :],
                "lam_init": 0.8 - 0.6 * math.exp(-0.3 * l), "cc": cc, "sc": sc, "wb": wb_b[l], "wo": wo_b[l],
                "lng0": ln_g[l, 0][None, :], "lnb0": ln_b[l, 0][None, :], "lng1": ln_g[l, 1][None, :],
                "lnb1": ln_b[l, 1][None, :], "wq": wq_b[l], "skc": skc[l], "u": u_b[l], "vt": vt_b[l]}

    xp = x_prompt.reshape(batch * seq, d)
    ctx_idx = lambda i, bps: dbatch
    kv = [[], [], [], []]
    for l in range(depth):
        mod3 = mod_all[l].reshape(rows, 1, 6 * d)
        xp, own = _layer(xp, batch, seq, mod3, ctx_idx, layer_weights(l), None, None, dft_ctx, True)
        for lst, a in zip(kv, own):
            lst.append(a)
    new_diff_k = jnp.stack(kv[0], 0).reshape(depth, batch, seq, DIFF_HEADS, 2 * DIFF_HEAD_DIM).swapaxes(0, 1)
    new_diff_v = jnp.stack(kv[1], 0).reshape(depth, batch, seq, DIFF_HEADS, 2 * DIFF_HEAD_DIM).swapaxes(0, 1)
    new_gqa_k = jnp.stack(kv[2], 0).reshape(depth, batch, seq, GQA_KV_HEADS, GQA_HEAD_DIM).swapaxes(0, 1)
    new_gqa_v = jnp.stack(kv[3], 0).reshape(depth, batch, seq, GQA_KV_HEADS, GQA_HEAD_DIM).swapaxes(0, 1)

    xs = x_sample.reshape(dbatch * dseq, d)
    lat_idx = lambda i, bps: i // bps
    for l in range(depth):
        mod3 = mod_all[l].reshape(rows, 1, 6 * d)
        ctx_kv = (cache_diff_k[:, l].reshape(dbatch, past, 512).astype(BF16),
                  cache_diff_v[:, l].reshape(dbatch, past, 512).astype(BF16),
                  cache_gqa_k[:, l].reshape(dbatch, past, 128).astype(BF16),
                  cache_gqa_v[:, l].reshape(dbatch, past, 128).astype(BF16))
        xs, _ = _layer(xs, dbatch, dseq, mod3, lat_idx, layer_weights(l), rope_tabs, ctx_kv, dft_lat, False)

    return (xp.reshape(batch, seq, d), xs.reshape(dbatch, dseq, d), new_diff_k, new_diff_v, new_gqa_k, new_gqa_v)
```

```python
import functools
import math

import numpy as np
import jax
import jax.numpy as jnp
from jax import lax
from jax.experimental import pallas as pl
from jax.experimental.pallas import tpu as pltpu

F32 = jnp.float32
BF16 = jnp.bfloat16

D_MODEL = 1024
GRID_W = 64
ROPE_THETA = 10000.0
DIFF_HEADS = 4
DIFF_HEAD_DIM = 64
GQA_KV_HEADS = 2
GQA_HEAD_DIM = 64
FOURIER_GROUP_DIM = 128
FOURIER_GROUPS = 4
N_BRANCH = 3
BRANCH_WIDTH = 512
C_DQ, C_DK, C_DV, C_GQ, C_GK, C_GV, C_FZ, C_GZ, C_END = 0, 512, 1024, 1536, 2048, 2176, 2304, 2816, 5888
PEER_HEADS = 8
PEER_KEYS = 128
PEER_EXPERTS = PEER_KEYS * PEER_KEYS
PEER_TOPK = 16
DEPTH_FOR_NORM = 4
DEEPNORM_ALPHA = (2 * DEPTH_FOR_NORM) ** 0.25
LN_EPS = 1e-6
ATTN_SCALE = 0.125 * math.log2(math.e)
SQRT_HALF = math.sqrt(0.5)
GELU_HALF = 0.5

LANES = 128
VMEM_LIMIT = 56 * 1024 * 1024
NEG_INF = float("-inf")


def _cparams(sem):
    return pltpu.CompilerParams(dimension_semantics=sem, vmem_limit_bytes=VMEM_LIMIT)


def _ln(x):
    mu = jnp.mean(x, -1, keepdims=True)
    xc = x - mu
    return xc * lax.rsqrt(jnp.mean(xc * xc, -1, keepdims=True) + LN_EPS)


def _dot(a, b):
    return jnp.dot(a, b, preferred_element_type=F32)


def _dot_nt(a, b):
    return lax.dot_general(a, b, (((1,), (1,)), ((), ())), preferred_element_type=F32)


def _mod_kernel(c_ref, w_ref, b_ref, o_ref):
    c = c_ref[...]
    a = c * jax.nn.sigmoid(c)
    o_ref[0] = jnp.dot(a, w_ref[0], preferred_element_type=F32, precision=lax.Precision.HIGHEST) + b_ref[0]


def _mod_call(cs, w_mod, b_mod):
    depth, d, n = w_mod.shape
    rows = cs.shape[0]
    nt = 1536
    return pl.pallas_call(
        _mod_kernel,
        grid=(depth, n // nt),
        in_specs=[pl.BlockSpec((rows, d), lambda l, j: (0, 0)),
                  pl.BlockSpec((1, d, nt), lambda l, j: (l, 0, j)),
                  pl.BlockSpec((1, 1, nt), lambda l, j: (l, 0, j))],
        out_specs=pl.BlockSpec((1, rows, nt), lambda l, j: (l, 0, j)),
        out_shape=jax.ShapeDtypeStruct((depth, rows, n), F32),
        compiler_params=_cparams(("arbitrary", "arbitrary")),
        name="mod_proj",
    )(cs, w_mod, b_mod.reshape(depth, 1, n))


def _rms64(y, bd, g):
    y2 = y * y
    hi = y2.astype(BF16)
    lo = (y2 - hi.astype(F32)).astype(BF16)
    ms = _dot(hi, bd) + _dot(lo, bd)
    return y * lax.rsqrt(ms + LN_EPS) * g


def _rope(y, cos, sin_signed):
    w = y.shape[1]
    reps = w // LANES
    c = jnp.concatenate([cos] * reps, axis=1) if reps > 1 else cos
    s = jnp.concatenate([sin_signed] * reps, axis=1) if reps > 1 else sin_signed
    lane = lax.broadcasted_iota(jnp.int32, y.shape, 1)
    even = (lane & 1) == 0
    partner = jnp.where(even, pltpu.roll(y, w - 1, 1), pltpu.roll(y, 1, 1))
    return y * c + partner * s


def _in_proj_kernel(*refs, rope, emit_kv):
    x_ref, mod_ref, w_ref, qg_ref, kg_ref, bd_ref = refs[:6]
    pos = 6
    if rope:
        cos_ref, sin_ref = refs[6:8]
        pos = 8
    dq_ref, dk_ref, dv_ref, gq_ref, gk_ref, gv_ref, fz_ref, gate_ref = refs[pos:pos + 8]
    pos += 8
    if emit_kv:
        ndk_ref, ndv_ref, ngk_ref, ngv_ref = refs[pos:pos + 4]

    mod = mod_ref[0]
    sh1 = mod[:, 0:D_MODEL]
    sc1 = mod[:, D_MODEL:2 * D_MODEL]
    h = (_ln(x_ref[...]) * (1.0 + sc1) + sh1).astype(BF16)

    def seg(lo, hi):
        return _dot(h, w_ref[:, lo:hi])

    if rope:
        cos = cos_ref[...]
        sin = sin_ref[...]
        rot = lambda y: _rope(y, cos, sin)
    else:
        rot = lambda y: y

    dq_ref[...] = (rot(seg(C_DQ, C_DK)) * ATTN_SCALE).astype(BF16)
    dk = seg(C_DK, C_DV)
    dk_ref[...] = rot(dk).astype(BF16)
    dv = seg(C_DV, C_GQ)
    dv_ref[...] = dv.astype(BF16)
    gq = _rms64(seg(C_GQ, C_GK), bd_ref[...], qg_ref[...])
    gq_ref[...] = (rot(gq) * ATTN_SCALE).astype(BF16)
    gk = _rms64(seg(C_GK, C_GV), bd_ref[0:LANES, 0:LANES], kg_ref[...])
    gk_ref[...] = rot(gk).astype(BF16)
    gv = seg(C_GV, C_FZ)
    gv_ref[...] = gv.astype(BF16)
    fz_ref[...] = seg(C_FZ, C_GZ).astype(BF16)
    for n in range(N_BRANCH):
        lo = C_GZ + n * D_MODEL
        gate_ref[:, n * D_MODEL:(n + 1) * D_MODEL] = jax.nn.sigmoid(seg(lo, lo + D_MODEL))
    if emit_kv:
        ndk_ref[...] = dk
        ndv_ref[...] = dv
        ngk_ref[...] = gk
        ngv_ref[...] = gv


def _in_proj_call(x, mod3, mod_idx, w_in, qg, kg, bd, rope_tabs, seq, tt, emit_kv):
    t, d = x.shape
    rope = rope_tabs is not None
    nblk_seq = seq // tt
    full = lambda shape: pl.BlockSpec(shape, lambda i: (0,) * len(shape))
    tok = lambda w: pl.BlockSpec((tt, w), lambda i: (i, 0))
    in_specs = [tok(d),
                pl.BlockSpec((1, 1, mod3.shape[2]), lambda i: (mod_idx(i), 0, 0)),
                pl.BlockSpec(w_in.shape, lambda i: (0, 0), pipeline_mode=pl.Buffered(1)),
                full(qg.shape), full(kg.shape), full(bd.shape)]
    args = [x, mod3, w_in, qg, kg, bd]
    if rope:
        in_specs += [pl.BlockSpec((tt, LANES), lambda i: (i % nblk_seq, 0))] * 2
        args += list(rope_tabs)
    widths = [512, 512, 512, 512, 128, 128, 512]
    out_specs = [tok(w) for w in widths] + [tok(N_BRANCH * d)]
    out_shape = [jax.ShapeDtypeStruct((t, w), BF16) for w in widths] + [jax.ShapeDtypeStruct((t, N_BRANCH * d), F32)]
    if emit_kv:
        for w in (512, 512, 128, 128):
            out_specs.append(tok(w))
            out_shape.append(jax.ShapeDtypeStruct((t, w), F32))
    return pl.pallas_call(
        functools.partial(_in_proj_kernel, rope=rope, emit_kv=emit_kv),
        grid=(t // tt,),
        in_specs=in_specs,
        out_specs=out_specs,
        out_shape=out_shape,
        compiler_params=_cparams(("arbitrary",)),
        name="in_proj",
    )(*args)


def _attn_kernel(dq_ref, k_ref, v_ref, gq_ref, gk_ref, gv_ref, lam_ref, dng_ref, do_ref, go_ref, *, lam_init):
    tq = dq_ref.shape[1]
    lane = lax.broadcasted_iota(jnp.int32, (tq, LANES), 1)
    is_lo = lane < DIFF_HEAD_DIM
    m_lo = is_lo.astype(F32)
    m_hi = 1.0 - m_lo

    def split_halves(q):
        qf = q.astype(F32)
        return jnp.concatenate([(qf * m_lo).astype(BF16), (qf * m_hi).astype(BF16)], axis=0)

    lv = lam_ref[...]
    lam = (jnp.exp(jnp.sum(lv[0:1] * lv[1:2], keepdims=True))
           - jnp.exp(jnp.sum(lv[2:3] * lv[3:4], keepdims=True)) + lam_init)
    dng = dng_ref[...]

    for h in range(DIFF_HEADS):
        sl = slice(h * LANES, (h + 1) * LANES)
        s = _dot_nt(split_halves(dq_ref[0, :, sl]), k_ref[0, :, sl])
        e = jnp.exp2(s - jnp.max(s, -1, keepdims=True))
        r = 1.0 / jnp.sum(e, -1, keepdims=True)
        w = (e[:tq] * r[:tq] - e[tq:] * (lam * r[tq:])).astype(BF16)
        o = _dot(w, v_ref[0, :, sl])
        o = o * lax.rsqrt(jnp.mean(o * o, -1, keepdims=True) + LN_EPS) * dng * (1.0 - lam_init)
        do_ref[0, :, sl] = o.astype(BF16)

    for hk in range(GQA_KV_HEADS):
        ksl = slice(hk * LANES, (hk + 1) * LANES)
        for m in range(2):
            c0 = (hk * 2 + m) * LANES
            sl = slice(c0, c0 + LANES)
            s = _dot_nt(split_halves(gq_ref[0, :, sl]), gk_ref[0, :, ksl])
            e = jnp.exp2(s - jnp.max(s, -1, keepdims=True))
            o = _dot(e.astype(BF16), gv_ref[0, :, ksl])
            o = o / o[:, GQA_HEAD_DIM:GQA_HEAD_DIM + 1]
            o_hi = pltpu.roll(o[tq:], GQA_HEAD_DIM, 1)
            go_ref[0, :, sl] = jnp.where(is_lo, o[:tq], o_hi).astype(BF16)


def _attn_call(dq, k, v, gq, gk, gv, lamv, dng, lam_init, tq):
    b, s, w = dq.shape
    sk = k.shape[1]
    qspec = pl.BlockSpec((1, tq, w), lambda bi, i: (bi, i, 0))
    kvspec = lambda width: pl.BlockSpec((1, sk, width), lambda bi, i: (bi, 0, 0))
    full = lambda shape: pl.BlockSpec(shape, lambda bi, i: (0,) * len(shape))
    return pl.pallas_call(
        functools.partial(_attn_kernel, lam_init=lam_init),
        grid=(b, s // tq),
        in_specs=[qspec, kvspec(512), kvspec(512), qspec, kvspec(256), kvspec(256),
                  full(lamv.shape), full(dng.shape)],
        out_specs=[qspec, qspec],
        out_shape=[jax.ShapeDtypeStruct((b, s, w), BF16)] * 2,
        compiler_params=_cparams(("arbitrary", "arbitrary")),
        name="attention",
    )(dq, k, v, gq, gk, gv, lamv, dng)


def _fourier_kernel(wc_ref, ws_ref, z_ref, cc_ref, sc_ref, o_ref, *, scale):
    z = z_ref[0]
    p = _dot(wc_ref[...], z).astype(BF16)
    q = _dot(ws_ref[...], z).astype(BF16)
    o_ref[0] = ((_dot(p, cc_ref[...]) - _dot(q, sc_ref[...])) * scale).astype(BF16)


def _fourier_call(z, wc, ws, cc, sc, tm):
    b, s, w = z.shape
    scale = 1.0 / math.sqrt(s * FOURIER_GROUP_DIM)
    return pl.pallas_call(
        functools.partial(_fourier_kernel, scale=scale),
        grid=(s // tm, b),
        in_specs=[pl.BlockSpec((tm, s), lambda i, bi: (i, 0)),
                  pl.BlockSpec((tm, s), lambda i, bi: (i, 0)),
                  pl.BlockSpec((1, s, w), lambda i, bi: (bi, 0, 0)),
                  pl.BlockSpec((w, w), lambda i, bi: (0, 0)),
                  pl.BlockSpec((w, w), lambda i, bi: (0, 0))],
        out_specs=pl.BlockSpec((1, tm, w), lambda i, bi: (bi, i, 0)),
        out_shape=jax.ShapeDtypeStruct((b, s, w), BF16),
        compiler_params=_cparams(("arbitrary", "arbitrary")),
        name="fourier",
    )(wc, ws, z, cc, sc)


def _angle_tables(rows, cols, period):
    m = (rows[:, None] * cols[None, :]) % period
    ang = m.astype(F32) * (2.0 * math.pi / period)
    return jnp.cos(ang), jnp.sin(ang)


def _dft_tables(n, split=64):
    j = jnp.arange(n, dtype=jnp.int32)
    if n <= 8 * split or n % split:
        return _angle_tables(j, j, n)
    n1 = n // split
    ca, sa = _angle_tables(j % n1, jnp.arange(n1, dtype=jnp.int32), n1)
    cb, sb = _angle_tables(j, jnp.arange(split, dtype=jnp.int32), n)
    cos = ca[:, :, None] * cb[:, None, :] - sa[:, :, None] * sb[:, None, :]
    sin = sa[:, :, None] * cb[:, None, :] + ca[:, :, None] * sb[:, None, :]
    return cos.reshape(n, n), sin.reshape(n, n)


def _merge_kernel(d_ref, g_ref, f_ref, gate_ref, x_ref, mod_ref, wb_ref, wo_ref, lng_ref, lnb_ref, o_ref):
    d = D_MODEL
    m = (gate_ref[:, 0:d] * _dot(d_ref[...], wb_ref[0])
         + gate_ref[:, d:2 * d] * _dot(g_ref[...], wb_ref[1])
         + gate_ref[:, 2 * d:3 * d] * _dot(f_ref[...], wb_ref[2]))
    mix = _dot(m.astype(BF16), wo_ref[...])
    g1 = mod_ref[0][:, 2 * d:3 * d]
    u = DEEPNORM_ALPHA * x_ref[...] + g1 * mix
    o_ref[...] = _ln(u) * lng_ref[...] + lnb_ref[...]


def _merge_call(d_o, g_o, f_o, gates, x, mod3, mod_idx, wb, wo, lng, lnb, tt):
    t, d = x.shape
    full = lambda shape: pl.BlockSpec(shape, lambda i: (0,) * len(shape))
    tok = lambda w: pl.BlockSpec((tt, w), lambda i: (i, 0))
    return pl.pallas_call(
        _merge_kernel,
        grid=(t // tt,),
        in_specs=[tok(512), tok(512), tok(512), tok(N_BRANCH * d), tok(d),
                  pl.BlockSpec((1, 1, mod3.shape[2]), lambda i: (mod_idx(i), 0, 0)),
                  full(wb.shape), full(wo.shape), full(lng.shape), full(lnb.shape)],
        out_specs=tok(d),
        out_shape=jax.ShapeDtypeStruct((t, d), F32),
        compiler_params=_cparams(("arbitrary",)),
        name="merge",
    )(d_o, g_o, f_o, gates, x, mod3, wb, wo, lng, lnb)


N_TOP = PEER_TOPK + 1
PAD_TOP = 24
N_CAND = PAD_TOP + 7 * 8 + (PAD_TOP - 8)


def _top_values(x, n):
    vals = []
    for _ in range(n):
        m = jnp.max(x, axis=0, keepdims=True)
        vals.append(m)
        x = jnp.where(x >= m, NEG_INF, x)
    return vals


def _route_head(s0, s1, av_ref, bv_ref, cand_ref):
    a = _top_values(s0, N_TOP)
    b = _top_values(s1, N_TOP)
    for k in range(N_TOP):
        av_ref[k:k + 1, :] = a[k]
        bv_ref[k:k + 1, :] = b[k]
    av = av_ref[...]
    bv = bv_ref[...]
    cand_ref[0:PAD_TOP, :] = a[0] + bv
    for k in range(1, 8):
        cand_ref[PAD_TOP + 8 * (k - 1):PAD_TOP + 8 * k, :] = a[k] + bv[0:8]
    cand_ref[PAD_TOP + 56:N_CAND, :] = av[8:PAD_TOP] + b[0]
    cand = cand_ref[...]
    c = _top_values(cand, N_TOP)
    tau = 0.5 * (c[PEER_TOPK - 1] + c[PEER_TOPK])
    z = jnp.sum(jnp.where(cand >= tau, jnp.exp(cand - c[0]), 0.0), axis=0, keepdims=True)
    w0 = jnp.exp(s0 - a[0]) * (GELU_HALF / z)
    w1 = jnp.exp(s1 - b[0])
    return -s0, s1 - tau, w0, w1


def _route_kernel(x_ref, mod_ref, wq_ref, sk_ref, hb_ref, ns0_ref, a1_ref, w0_ref, w1_ref,
                  av_ref, bv_ref, cand_ref):
    d = D_MODEL
    tt = x_ref.shape[0]
    mod = mod_ref[0]
    sh2 = mod[:, 3 * d:4 * d]
    sc2 = mod[:, 4 * d:5 * d]
    h = _ln(x_ref[...]) * (1.0 + sc2) + sh2
    hb = h.astype(BF16)
    ht = h.T.astype(BF16)
    for cb in range(tt // MM_COLS):
        hb_ref[cb] = ht[:, cb * MM_COLS:(cb + 1) * MM_COLS]
    q = _dot(hb, wq_ref[...])
    lane = lax.broadcasted_iota(jnp.int32, (tt, LANES), 1)
    m_lo = (lane < PEER_KEYS // 2).astype(F32)
    m_hi = 1.0 - m_lo
    av_ref[...] = jnp.full(av_ref.shape, NEG_INF, F32)
    bv_ref[...] = jnp.full(bv_ref.shape, NEG_INF, F32)
    for h in range(PEER_HEADS):
        qh = q[:, h * LANES:(h + 1) * LANES]
        sk = sk_ref[h]
        s0 = _dot_nt(sk, (qh * m_lo).astype(BF16))
        s1 = _dot_nt(sk, (qh * m_hi).astype(BF16))
        ns0, a1, w0, w1 = _route_head(s0, s1, av_ref, bv_ref, cand_ref)
        for tc in range(tt // LANES):
            cols = slice(tc * LANES, (tc + 1) * LANES)
            ns0_ref[h, tc] = ns0[:, cols]
            a1_ref[h, tc] = a1[:, cols]
            w0_ref[h, tc] = w0[:, cols]
            w1_ref[h, tc] = w1[:, cols]


def _route_call(x, mod3, mod_idx, wq, skc, tt):
    t, d = x.shape
    full = lambda shape: pl.BlockSpec(shape, lambda i: (0,) * len(shape))
    rspec = pl.BlockSpec((PEER_HEADS, tt // LANES, PEER_KEYS, LANES), lambda i: (0, i, 0, 0))
    rshape = jax.ShapeDtypeStruct((PEER_HEADS, t // LANES, PEER_KEYS, LANES), F32)
    return pl.pallas_call(
        _route_kernel,
        grid=(t // tt,),
        in_specs=[pl.BlockSpec((tt, d), lambda i: (i, 0)),
                  pl.BlockSpec((1, 1, mod3.shape[2]), lambda i: (mod_idx(i), 0, 0)),
                  full(wq.shape), full(skc.shape)],
        out_specs=[pl.BlockSpec((tt // MM_COLS, d, MM_COLS), lambda i: (i, 0, 0)), rspec, rspec, rspec, rspec],
        out_shape=[jax.ShapeDtypeStruct((t // MM_COLS, d, MM_COLS), BF16), rshape, rshape, rshape, rshape],
        scratch_shapes=[pltpu.VMEM((PAD_TOP, tt), F32), pltpu.VMEM((PAD_TOP, tt), F32),
                        pltpu.VMEM((N_CAND, tt), F32)],
        compiler_params=_cparams(("arbitrary",)),
        name="peer_route",
    )(x, mod3, wq, skc)


MM_COLS = 256
G_ROWS = 32
G_REUSE = 8
PEER_CHUNK = 1024


def _peer_kernel(hb_ref, ns0_ref, a1_ref, w0_ref, w1_ref, u_ref, vt_ref, x_ref, mod_ref, lng_ref, lnb_ref,
                 o_ref, acc_ref, a_ref, ga_ref, *, ni):
    c = pl.program_id(1)
    n_cb = hb_ref.shape[0]
    n_tc = ns0_ref.shape[1]

    @pl.when(c == 0)
    def _():
        acc_ref[...] = jnp.zeros_like(acc_ref)

    for cb in range(n_cb):
        a_blk = _dot(u_ref[...], hb_ref[cb])
        a_ref[2 * cb] = a_blk[:, :LANES]
        a_ref[2 * cb + 1] = a_blk[:, LANES:]

    i_base = pl.multiple_of(c * ni, ni)
    n_jq = PEER_KEYS // G_ROWS

    def gate_column(tc, carry):
        for ig in range(ni // G_REUSE):
            i0 = pl.multiple_of(i_base + ig * G_REUSE, G_REUSE)
            nrows = [ns0_ref[h, tc, pl.ds(i0, G_REUSE), :] for h in range(PEER_HEADS)]
            wrows = [w0_ref[h, tc, pl.ds(i0, G_REUSE), :] for h in range(PEER_HEADS)]
            bn = [[jnp.broadcast_to(nrows[h][r:r + 1], (G_ROWS, LANES)) for r in range(G_REUSE)]
                  for h in range(PEER_HEADS)]
            bw = [[jnp.broadcast_to(wrows[h][r:r + 1], (G_ROWS, LANES)) for r in range(G_REUSE)]
                  for h in range(PEER_HEADS)]
            for jq in range(n_jq):
                jrows = slice(jq * G_ROWS, (jq + 1) * G_ROWS)
                g = [None] * G_REUSE
                for h in range(PEER_HEADS):
                    a1q = a1_ref[h, tc, jrows, :]
                    w1q = w1_ref[h, tc, jrows, :]
                    for r in range(G_REUSE):
                        term = jnp.where(a1q >= bn[h][r], w1q * bw[h][r], 0.0)
                        g[r] = term if g[r] is None else g[r] + term
                for r in range(G_REUSE):
                    rows = slice((ig * G_REUSE + r) * PEER_KEYS + jq * G_ROWS,
                                 (ig * G_REUSE + r) * PEER_KEYS + (jq + 1) * G_ROWS)
                    a = a_ref[tc, rows, :]
                    act = a * (1.0 + lax.erf(a * SQRT_HALF))
                    ga_ref[tc, rows, :] = (g[r] * act).astype(BF16)
        return carry

    lax.fori_loop(0, n_tc, gate_column, 0)

    ga = jnp.concatenate([ga_ref[tc] for tc in range(n_tc)], axis=1)
    acc_ref[...] += _dot(vt_ref[...], ga)

    @pl.when(c == pl.num_programs(1) - 1)
    def _():
        d = D_MODEL
        g2 = mod_ref[0][:, 5 * d:6 * d]
        u = DEEPNORM_ALPHA * x_ref[...] + g2 * acc_ref[...].T
        o_ref[...] = _ln(u) * lng_ref[...] + lnb_ref[...]


def _peer_call(hb, ns0, a1, w0, w1, u_tab, vt_tab, x, mod3, mod_idx, lng, lnb, tt, ec):
    t, d = x.shape
    ne = u_tab.shape[0]
    ni = ec // PEER_KEYS
    full = lambda shape: pl.BlockSpec(shape, lambda i, c: (0,) * len(shape))
    rspec = pl.BlockSpec((PEER_HEADS, tt // LANES, PEER_KEYS, LANES), lambda i, c: (0, i, 0, 0))
    return pl.pallas_call(
        functools.partial(_peer_kernel, ni=ni),
        grid=(t // tt, ne // ec),
        in_specs=[pl.BlockSpec((tt // MM_COLS, d, MM_COLS), lambda i, c: (i, 0, 0)), rspec, rspec, rspec, rspec,
                  pl.BlockSpec((ec, d), lambda i, c: (c, 0)),
                  pl.BlockSpec((d, ec), lambda i, c: (0, c)),
                  pl.BlockSpec((tt, d), lambda i, c: (i, 0)),
                  pl.BlockSpec((1, 1, mod3.shape[2]), lambda i, c: (mod_idx(i), 0, 0)),
                  full(lng.shape), full(lnb.shape)],
        out_specs=pl.BlockSpec((tt, d), lambda i, c: (i, 0)),
        out_shape=jax.ShapeDtypeStruct((t, d), F32),
        scratch_shapes=[pltpu.VMEM((d, tt), F32), pltpu.VMEM((tt // LANES, ec, LANES), F32),
                        pltpu.VMEM((tt // LANES, ec, LANES), BF16)],
        compiler_params=_cparams(("arbitrary", "arbitrary")),
        name="peer_experts",
    )(hb, ns0, a1, w0, w1, u_tab, vt_tab, x, mod3, lng, lnb)


TILE_IN_PROJ = 256
TILE_TOKENS = 512
TILE_ROUTE = 256
TILE_FOURIER = 512
TILE_Q_LONG = 128
TILE_Q_SHORT = 256
LONG_KEYS = 1024


def _pick_tile(n, want):
    t = min(n, want)
    while n % t:
        t //= 2
    return t


def _rope_tables(seq):
    n_rows = seq // GRID_W
    row = jnp.repeat(jnp.arange(n_rows), GRID_W).astype(F32)
    col = jnp.tile(jnp.arange(GRID_W), n_rows).astype(F32)
    n_freq = DIFF_HEAD_DIM // 4
    freqs = ROPE_THETA ** (-jnp.arange(n_freq, dtype=F32) / n_freq)
    ang = jnp.concatenate([row[:, None] * freqs, col[:, None] * freqs], -1)
    cos = jnp.repeat(jnp.cos(ang), 2, axis=-1)
    sin = jnp.repeat(jnp.sin(ang), 2, axis=-1)
    sign = jnp.where(jnp.arange(DIFF_HEAD_DIM) % 2 == 0, -1.0, 1.0).astype(F32)
    reps = LANES // DIFF_HEAD_DIM
    return jnp.tile(cos, (1, reps)), jnp.tile(sin * sign, (1, reps))


def _dup_heads(a):
    b, s, _ = a.shape
    a4 = a.reshape(b, s, GQA_KV_HEADS, 1, GQA_HEAD_DIM)
    return jnp.broadcast_to(a4, (b, s, GQA_KV_HEADS, 2, GQA_HEAD_DIM)).reshape(b, s, 4 * GQA_HEAD_DIM)


def _values_with_ones(a):
    b, s, _ = a.shape
    a4 = a.reshape(b, s, GQA_KV_HEADS, 1, GQA_HEAD_DIM)
    return jnp.concatenate([a4, jnp.ones_like(a4)], axis=3).reshape(b, s, 4 * GQA_HEAD_DIM)


def _layer(x, batch, seq, mod3, mod_idx_fn, lw, rope_tabs, ctx_kv, dft, emit_kv):
    t = batch * seq
    tile = lambda want: _pick_tile(seq if rope_tabs is not None else t, want)
    tt = tile(TILE_IN_PROJ)
    mod_idx = lambda i: mod_idx_fn(i, max(seq // tt, 1))
    outs = _in_proj_call(x, mod3, mod_idx, lw["w_in"], lw["qg"], lw["kg"], lw["bd"], rope_tabs, seq, tt, emit_kv)
    dq, dk, dv, gq, gk, gv, fz, gates = outs[:8]
    r3 = lambda a: a.reshape(batch, seq, a.shape[-1])
    k_att, v_att, gk_att, gv_att = r3(dk), r3(dv), r3(gk), r3(gv)
    if ctx_kv is not None:
        cdk, cdv, cgk, cgv = ctx_kv
        k_att = jnp.concatenate([k_att, cdk], axis=1)
        v_att = jnp.concatenate([v_att, cdv], axis=1)
        gk_att = jnp.concatenate([gk_att, cgk], axis=1)
        gv_att = jnp.concatenate([gv_att, cgv], axis=1)
    tq = _pick_tile(seq, TILE_Q_LONG if k_att.shape[1] > LONG_KEYS else TILE_Q_SHORT)
    d_o, g_o = _attn_call(r3(dq), k_att, v_att, r3(gq), _dup_heads(gk_att), _values_with_ones(gv_att),
                          lw["diff_lam"], lw["dng"], lw["lam_init"], tq)
    f_o = _fourier_call(r3(fz), dft[0], dft[1], lw["cc"], lw["sc"], _pick_tile(seq, TILE_FOURIER))
    flat = lambda a: a.reshape(t, a.shape[-1])
    tm = tile(TILE_TOKENS)
    mod_idx_m = lambda i: mod_idx_fn(i, max(seq // tm, 1))
    x1 = _merge_call(flat(d_o), flat(g_o), flat(f_o), gates, x, mod3, mod_idx_m, lw["wb"], lw["wo"],
                     lw["lng0"], lw["lnb0"], tm)
    tr = tile(TILE_ROUTE)
    mod_idx_r = lambda i: mod_idx_fn(i, max(seq // tr, 1))
    hb, ns0, a1, w0, w1 = _route_call(x1, mod3, mod_idx_r, lw["wq"], lw["skc"], tr)
    x2 = _peer_call(hb, ns0, a1, w0, w1, lw["u"], lw["vt"], x1, mod3, mod_idx_m, lw["lng1"], lw["lnb1"], tm, PEER_CHUNK)
    return x2, outs[8:]


def kernel(x_prompt, x_sample, cache_diff_k, cache_diff_v, cache_gqa_k, cache_gqa_v, c, c_ctx, w_mod, b_mod, w_in,
           diff_lam, diff_norm_g, q_norm_g, k_norm_g, w_branch, w_out, ln_g, ln_b, peer_wq, peer_subkeys, peer_u,
           peer_v):
    depth = w_in.shape[0]
    batch, seq, d = x_prompt.shape
    dbatch, dseq, _ = x_sample.shape
    past = cache_diff_k.shape[2]

    rows = ((dbatch + 1 + 7) // 8) * 8
    cs = jnp.zeros((rows, d), F32).at[:dbatch].set(c).at[dbatch].set(c_ctx)
    mod_all = _mod_call(cs, w_mod, b_mod)

    w_in_b = w_in.astype(BF16)
    wb_b = w_branch.astype(BF16)
    wo_b = w_out.astype(BF16)
    wq_b = peer_wq.astype(BF16)
    u_b = peer_u.astype(BF16)
    vt_b = jnp.swapaxes(peer_v, 1, 2).astype(BF16)
    skc = jnp.transpose(peer_subkeys, (0, 1, 3, 2, 4)).reshape(depth, PEER_HEADS, PEER_KEYS, PEER_KEYS).astype(BF16)
    bd = jnp.asarray(np.kron(np.eye(8, dtype=np.float32), np.full((64, 64), 1.0 / 64, np.float32)), BF16)
    ck, sk_ = _dft_tables(FOURIER_GROUP_DIM)
    eye_g = jnp.eye(FOURIER_GROUPS, dtype=F32)
    cc = jnp.kron(eye_g, ck).astype(BF16)
    sc = jnp.kron(eye_g, sk_).astype(BF16)
    dft_ctx = tuple(a.astype(BF16) for a in _dft_tables(seq))
    dft_lat = tuple(a.astype(BF16) for a in _dft_tables(dseq))
    rope_tabs = _rope_tables(dseq)

    def layer_weights(l):
        return {"w_in": w_in_b[l], "qg": jnp.tile(q_norm_g[l], 8)[None, :], "kg": jnp.tile(k_norm_g[l], 2)[None, :],
                "bd": bd, "diff_lam": diff_lam[l], "dng": diff_norm_g[l][None, :],
                "lam_init": 0.8 - 0.6 * math.exp(-0.3 * l), "cc": cc, "sc": sc, "wb": wb_b[l], "wo": wo_b[l],
                "lng0": ln_g[l, 0][None, :], "lnb0": ln_b[l, 0][None, :], "lng1": ln_g[l, 1][None, :],
                "lnb1": ln_b[l, 1][None, :], "wq": wq_b[l], "skc": skc[l], "u": u_b[l], "vt": vt_b[l]}

    xp = x_prompt.reshape(batch * seq, d)
    ctx_idx = lambda i, bps: dbatch
    kv = [[], [], [], []]
    for l in range(depth):
        mod3 = mod_all[l].reshape(rows, 1, 6 * d)
        xp, own = _layer(xp, batch, seq, mod3, ctx_idx, layer_weights(l), None, None, dft_ctx, True)
        for lst, a in zip(kv, own):
            lst.append(a)
    new_diff_k = jnp.stack(kv[0], 0).reshape(depth, batch, seq, DIFF_HEADS, 2 * DIFF_HEAD_DIM).swapaxes(0, 1)
    new_diff_v = jnp.stack(kv[1], 0).reshape(depth, batch, seq, DIFF_HEADS, 2 * DIFF_HEAD_DIM).swapaxes(0, 1)
    new_gqa_k = jnp.stack(kv[2], 0).reshape(depth, batch, seq, GQA_KV_HEADS, GQA_HEAD_DIM).swapaxes(0, 1)
    new_gqa_v = jnp.stack(kv[3], 0).reshape(depth, batch, seq, GQA_KV_HEADS, GQA_HEAD_DIM).swapaxes(0, 1)

    xs = x_sample.reshape(dbatch * dseq, d)
    lat_idx = lambda i, bps: i // bps
    for l in range(depth):
        mod3 = mod_all[l].reshape(rows, 1, 6 * d)
        ctx_kv = (cache_diff_k[:, l].reshape(dbatch, past, 512).astype(BF16),
                  cache_diff_v[:, l].reshape(dbatch, past, 512).astype(BF16),
                  cache_gqa_k[:, l].reshape(dbatch, past, 128).astype(BF16),
                  cache_gqa_v[:, l].reshape(dbatch, past, 128).astype(BF16))
        xs, _ = _layer(xs, dbatch, dseq, mod3, lat_idx, layer_weights(l), rope_tabs, ctx_kv, dft_lat, False)

    return (xp.reshape(batch, seq, d), xs.reshape(dbatch, dseq, d), new_diff_k, new_diff_v, new_gqa_k, new_gqa_v)
```

```python
import functools
import math

import numpy as np
import jax
import jax.numpy as jnp
from jax import lax
from jax.experimental import pallas as pl
from jax.experimental.pallas import tpu as pltpu

F32 = jnp.float32
BF16 = jnp.bfloat16

D_MODEL = 1024
GRID_W = 64
ROPE_THETA = 10000.0
DIFF_HEADS = 4
DIFF_HEAD_DIM = 64
GQA_KV_HEADS = 2
GQA_HEAD_DIM = 64
FOURIER_GROUP_DIM = 128
FOURIER_GROUPS = 4
N_BRANCH = 3
BRANCH_WIDTH = 512
C_DQ, C_DK, C_DV, C_GQ, C_GK, C_GV, C_FZ, C_GZ, C_END = 0, 512, 1024, 1536, 2048, 2176, 2304, 2816, 5888
PEER_HEADS = 8
PEER_KEYS = 128
PEER_EXPERTS = PEER_KEYS * PEER_KEYS
PEER_TOPK = 16
DEPTH_FOR_NORM = 4
DEEPNORM_ALPHA = (2 * DEPTH_FOR_NORM) ** 0.25
LN_EPS = 1e-6
ATTN_SCALE = 0.125 * math.log2(math.e)
SQRT_HALF = math.sqrt(0.5)
GELU_HALF = 0.5

LANES = 128
VMEM_LIMIT = 56 * 1024 * 1024
NEG_INF = float("-inf")


def _cparams(sem):
    return pltpu.CompilerParams(dimension_semantics=sem, vmem_limit_bytes=VMEM_LIMIT)


def _ln(x):
    mu = jnp.mean(x, -1, keepdims=True)
    xc = x - mu
    return xc * lax.rsqrt(jnp.mean(xc * xc, -1, keepdims=True) + LN_EPS)


def _dot(a, b):
    return jnp.dot(a, b, preferred_element_type=F32)


def _dot_nt(a, b):
    return lax.dot_general(a, b, (((1,), (1,)), ((), ())), preferred_element_type=F32)


def _mod_kernel(c_ref, w_ref, b_ref, o_ref):
    c = c_ref[...]
    a = c * jax.nn.sigmoid(c)
    o_ref[0] = jnp.dot(a, w_ref[0], preferred_element_type=F32, precision=lax.Precision.HIGHEST) + b_ref[0]


def _mod_call(cs, w_mod, b_mod):
    depth, d, n = w_mod.shape
    rows = cs.shape[0]
    nt = 1536
    return pl.pallas_call(
        _mod_kernel,
        grid=(depth, n // nt),
        in_specs=[pl.BlockSpec((rows, d), lambda l, j: (0, 0)),
                  pl.BlockSpec((1, d, nt), lambda l, j: (l, 0, j)),
                  pl.BlockSpec((1, 1, nt), lambda l, j: (l, 0, j))],
        out_specs=pl.BlockSpec((1, rows, nt), lambda l, j: (l, 0, j)),
        out_shape=jax.ShapeDtypeStruct((depth, rows, n), F32),
        compiler_params=_cparams(("arbitrary", "arbitrary")),
        name="mod_proj",
    )(cs, w_mod, b_mod.reshape(depth, 1, n))


def _rms64(y, bd, g):
    y2 = y * y
    hi = y2.astype(BF16)
    lo = (y2 - hi.astype(F32)).astype(BF16)
    ms = _dot(hi, bd) + _dot(lo, bd)
    return y * lax.rsqrt(ms + LN_EPS) * g


def _rope(y, cos, sin_signed):
    w = y.shape[1]
    reps = w // LANES
    c = jnp.concatenate([cos] * reps, axis=1) if reps > 1 else cos
    s = jnp.concatenate([sin_signed] * reps, axis=1) if reps > 1 else sin_signed
    lane = lax.broadcasted_iota(jnp.int32, y.shape, 1)
    even = (lane & 1) == 0
    partner = jnp.where(even, pltpu.roll(y, w - 1, 1), pltpu.roll(y, 1, 1))
    return y * c + partner * s


def _in_proj_kernel(*refs, rope, emit_kv):
    x_ref, mod_ref, w_ref, qg_ref, kg_ref, bd_ref = refs[:6]
    pos = 6
    if rope:
        cos_ref, sin_ref = refs[6:8]
        pos = 8
    dq_ref, dk_ref, dv_ref, gq_ref, gk_ref, gv_ref, fz_ref, gate_ref = refs[pos:pos + 8]
    pos += 8
    if emit_kv:
        ndk_ref, ndv_ref, ngk_ref, ngv_ref = refs[pos:pos + 4]

    mod = mod_ref[0]
    sh1 = mod[:, 0:D_MODEL]
    sc1 = mod[:, D_MODEL:2 * D_MODEL]
    h = (_ln(x_ref[...]) * (1.0 + sc1) + sh1).astype(BF16)

    def seg(lo, hi):
        return _dot(h, w_ref[:, lo:hi])

    if rope:
        cos = cos_ref[...]
        sin = sin_ref[...]
        rot = lambda y: _rope(y, cos, sin)
    else:
        rot = lambda y: y

    dq_ref[...] = (rot(seg(C_DQ, C_DK)) * ATTN_SCALE).astype(BF16)
    dk = seg(C_DK, C_DV)
    dk_ref[...] = rot(dk).astype(BF16)
    dv = seg(C_DV, C_GQ)
    dv_ref[...] = dv.astype(BF16)
    gq = _rms64(seg(C_GQ, C_GK), bd_ref[...], qg_ref[...])
    gq_ref[...] = (rot(gq) * ATTN_SCALE).astype(BF16)
    gk = _rms64(seg(C_GK, C_GV), bd_ref[0:LANES, 0:LANES], kg_ref[...])
    gk_ref[...] = rot(gk).astype(BF16)
    gv = seg(C_GV, C_FZ)
    gv_ref[...] = gv.astype(BF16)
    fz_ref[...] = seg(C_FZ, C_GZ).astype(BF16)
    for n in range(N_BRANCH):
        lo = C_GZ + n * D_MODEL
        gate_ref[:, n * D_MODEL:(n + 1) * D_MODEL] = jax.nn.sigmoid(seg(lo, lo + D_MODEL))
    if emit_kv:
        ndk_ref[...] = dk
        ndv_ref[...] = dv
        ngk_ref[...] = gk
        ngv_ref[...] = gv


def _in_proj_call(x, mod3, mod_idx, w_in, qg, kg, bd, rope_tabs, seq, tt, emit_kv):
    t, d = x.shape
    rope = rope_tabs is not None
    nblk_seq = seq // tt
    full = lambda shape: pl.BlockSpec(shape, lambda i: (0,) * len(shape))
    tok = lambda w: pl.BlockSpec((tt, w), lambda i: (i, 0))
    in_specs = [tok(d),
                pl.BlockSpec((1, 1, mod3.shape[2]), lambda i: (mod_idx(i), 0, 0)),
                pl.BlockSpec(w_in.shape, lambda i: (0, 0), pipeline_mode=pl.Buffered(1)),
                full(qg.shape), full(kg.shape), full(bd.shape)]
    args = [x, mod3, w_in, qg, kg, bd]
    if rope:
        in_specs += [pl.BlockSpec((tt, LANES), lambda i: (i % nblk_seq, 0))] * 2
        args += list(rope_tabs)
    widths = [512, 512, 512, 512, 128, 128, 512]
    out_specs = [tok(w) for w in widths] + [tok(N_BRANCH * d)]
    out_shape = [jax.ShapeDtypeStruct((t, w), BF16) for w in widths] + [jax.ShapeDtypeStruct((t, N_BRANCH * d), F32)]
    if emit_kv:
        for w in (512, 512, 128, 128):
            out_specs.append(tok(w))
            out_shape.append(jax.ShapeDtypeStruct((t, w), F32))
    return pl.pallas_call(
        functools.partial(_in_proj_kernel, rope=rope, emit_kv=emit_kv),
        grid=(t // tt,),
        in_specs=in_specs,
        out_specs=out_specs,
        out_shape=out_shape,
        compiler_params=_cparams(("arbitrary",)),
        name="in_proj",
    )(*args)


def _attn_kernel(dq_ref, k_ref, v_ref, gq_ref, gk_ref, gv_ref, lam_ref, dng_ref, do_ref, go_ref, *, lam_init):
    tq = dq_ref.shape[1]
    lane = lax.broadcasted_iota(jnp.int32, (tq, LANES), 1)
    is_lo = lane < DIFF_HEAD_DIM
    m_lo = is_lo.astype(F32)
    m_hi = 1.0 - m_lo

    def split_halves(q):
        qf = q.astype(F32)
        return jnp.concatenate([(qf * m_lo).astype(BF16), (qf * m_hi).astype(BF16)], axis=0)

    lv = lam_ref[...]
    lam = (jnp.exp(jnp.sum(lv[0:1] * lv[1:2], keepdims=True))
           - jnp.exp(jnp.sum(lv[2:3] * lv[3:4], keepdims=True)) + lam_init)
    dng = dng_ref[...]

    for h in range(DIFF_HEADS):
        sl = slice(h * LANES, (h + 1) * LANES)
        s = _dot_nt(split_halves(dq_ref[0, :, sl]), k_ref[0, :, sl])
        e = jnp.exp2(s - jnp.max(s, -1, keepdims=True))
        r = 1.0 / jnp.sum(e, -1, keepdims=True)
        pv = _dot(e.astype(BF16), v_ref[0, :, sl])
        o = pv[:tq] * r[:tq] - pv[tq:] * (lam * r[tq:])
        o = o * lax.rsqrt(jnp.mean(o * o, -1, keepdims=True) + LN_EPS) * dng * (1.0 - lam_init)
        do_ref[0, :, sl] = o.astype(BF16)

    for hk in range(GQA_KV_HEADS):
        ksl = slice(hk * LANES, (hk + 1) * LANES)
        for m in range(2):
            c0 = (hk * 2 + m) * LANES
            sl = slice(c0, c0 + LANES)
            s = _dot_nt(split_halves(gq_ref[0, :, sl]), gk_ref[0, :, ksl])
            e = jnp.exp2(s - jnp.max(s, -1, keepdims=True))
            o = _dot(e.astype(BF16), gv_ref[0, :, ksl])
            o = o / o[:, GQA_HEAD_DIM:GQA_HEAD_DIM + 1]
            o_hi = pltpu.roll(o[tq:], GQA_HEAD_DIM, 1)
            go_ref[0, :, sl] = jnp.where(is_lo, o[:tq], o_hi).astype(BF16)


def _attn_call(dq, k, v, gq, gk, gv, lamv, dng, lam_init, tq):
    b, s, w = dq.shape
    sk = k.shape[1]
    qspec = pl.BlockSpec((1, tq, w), lambda bi, i: (bi, i, 0))
    kvspec = lambda width: pl.BlockSpec((1, sk, width), lambda bi, i: (bi, 0, 0))
    full = lambda shape: pl.BlockSpec(shape, lambda bi, i: (0,) * len(shape))
    return pl.pallas_call(
        functools.partial(_attn_kernel, lam_init=lam_init),
        grid=(b, s // tq),
        in_specs=[qspec, kvspec(512), kvspec(512), qspec, kvspec(256), kvspec(256),
                  full(lamv.shape), full(dng.shape)],
        out_specs=[qspec, qspec],
        out_shape=[jax.ShapeDtypeStruct((b, s, w), BF16)] * 2,
        compiler_params=_cparams(("arbitrary", "arbitrary")),
        name="attention",
    )(dq, k, v, gq, gk, gv, lamv, dng)


def _fourier_kernel(wc_ref, ws_ref, z_ref, cc_ref, sc_ref, o_ref, *, scale):
    z = z_ref[0]
    p = _dot(wc_ref[...], z).astype(BF16)
    q = _dot(ws_ref[...], z).astype(BF16)
    o_ref[0] = ((_dot(p, cc_ref[...]) - _dot(q, sc_ref[...])) * scale).astype(BF16)


def _fourier_call(z, wc, ws, cc, sc, tm):
    b, s, w = z.shape
    scale = 1.0 / math.sqrt(s * FOURIER_GROUP_DIM)
    return pl.pallas_call(
        functools.partial(_fourier_kernel, scale=scale),
        grid=(s // tm, b),
        in_specs=[pl.BlockSpec((tm, s), lambda i, bi: (i, 0)),
                  pl.BlockSpec((tm, s), lambda i, bi: (i, 0)),
                  pl.BlockSpec((1, s, w), lambda i, bi: (bi, 0, 0)),
                  pl.BlockSpec((w, w), lambda i, bi: (0, 0)),
                  pl.BlockSpec((w, w), lambda i, bi: (0, 0))],
        out_specs=pl.BlockSpec((1, tm, w), lambda i, bi: (bi, i, 0)),
        out_shape=jax.ShapeDtypeStruct((b, s, w), BF16),
        compiler_params=_cparams(("arbitrary", "arbitrary")),
        name="fourier",
    )(wc, ws, z, cc, sc)


def _angle_tables(rows, cols, period):
    m = (rows[:, None] * cols[None, :]) % period
    ang = m.astype(F32) * (2.0 * math.pi / period)
    return jnp.cos(ang), jnp.sin(ang)


def _dft_tables(n, split=64):
    j = jnp.arange(n, dtype=jnp.int32)
    if n <= 8 * split or n % split:
        return _angle_tables(j, j, n)
    n1 = n // split
    ca, sa = _angle_tables(j % n1, jnp.arange(n1, dtype=jnp.int32), n1)
    cb, sb = _angle_tables(j, jnp.arange(split, dtype=jnp.int32), n)
    cos = ca[:, :, None] * cb[:, None, :] - sa[:, :, None] * sb[:, None, :]
    sin = sa[:, :, None] * cb[:, None, :] + ca[:, :, None] * sb[:, None, :]
    return cos.reshape(n, n), sin.reshape(n, n)


def _merge_kernel(d_ref, g_ref, f_ref, gate_ref, x_ref, mod_ref, wb_ref, wo_ref, lng_ref, lnb_ref, o_ref):
    d = D_MODEL
    m = (gate_ref[:, 0:d] * _dot(d_ref[...], wb_ref[0])
         + gate_ref[:, d:2 * d] * _dot(g_ref[...], wb_ref[1])
         + gate_ref[:, 2 * d:3 * d] * _dot(f_ref[...], wb_ref[2]))
    mix = _dot(m.astype(BF16), wo_ref[...])
    g1 = mod_ref[0][:, 2 * d:3 * d]
    u = DEEPNORM_ALPHA * x_ref[...] + g1 * mix
    o_ref[...] = _ln(u) * lng_ref[...] + lnb_ref[...]


def _merge_call(d_o, g_o, f_o, gates, x, mod3, mod_idx, wb, wo, lng, lnb, tt):
    t, d = x.shape
    full = lambda shape: pl.BlockSpec(shape, lambda i: (0,) * len(shape))
    tok = lambda w: pl.BlockSpec((tt, w), lambda i: (i, 0))
    return pl.pallas_call(
        _merge_kernel,
        grid=(t // tt,),
        in_specs=[tok(512), tok(512), tok(512), tok(N_BRANCH * d), tok(d),
                  pl.BlockSpec((1, 1, mod3.shape[2]), lambda i: (mod_idx(i), 0, 0)),
                  full(wb.shape), full(wo.shape), full(lng.shape), full(lnb.shape)],
        out_specs=tok(d),
        out_shape=jax.ShapeDtypeStruct((t, d), F32),
        compiler_params=_cparams(("arbitrary",)),
        name="merge",
    )(d_o, g_o, f_o, gates, x, mod3, wb, wo, lng, lnb)


N_TOP = PEER_TOPK + 1
PAD_TOP = 24
N_CAND = PAD_TOP + 7 * 8 + (PAD_TOP - 8)


def _top_values(x, n):
    vals = []
    for _ in range(n):
        m = jnp.max(x, axis=0, keepdims=True)
        vals.append(m)
        x = jnp.where(x >= m, NEG_INF, x)
    return vals


def _route_head(s0, s1, av_ref, bv_ref, cand_ref):
    a = _top_values(s0, N_TOP)
    b = _top_values(s1, N_TOP)
    for k in range(N_TOP):
        av_ref[k:k + 1, :] = a[k]
        bv_ref[k:k + 1, :] = b[k]
    av = av_ref[...]
    bv = bv_ref[...]
    cand_ref[0:PAD_TOP, :] = a[0] + bv
    for k in range(1, 8):
        cand_ref[PAD_TOP + 8 * (k - 1):PAD_TOP + 8 * k, :] = a[k] + bv[0:8]
    cand_ref[PAD_TOP + 56:N_CAND, :] = av[8:PAD_TOP] + b[0]
    cand = cand_ref[...]
    c = _top_values(cand, N_TOP)
    tau = 0.5 * (c[PEER_TOPK - 1] + c[PEER_TOPK])
    z = jnp.sum(jnp.where(cand >= tau, jnp.exp(cand - c[0]), 0.0), axis=0, keepdims=True)
    w0 = jnp.exp(s0 - a[0]) * (GELU_HALF / z)
    w1 = jnp.exp(s1 - b[0])
    return -s0, s1 - tau, w0, w1


def _route_kernel(x_ref, mod_ref, wq_ref, sk_ref, hb_ref, ns0_ref, a1_ref, w0_ref, w1_ref,
                  av_ref, bv_ref, cand_ref):
    d = D_MODEL
    tt = x_ref.shape[0]
    mod = mod_ref[0]
    sh2 = mod[:, 3 * d:4 * d]
    sc2 = mod[:, 4 * d:5 * d]
    h = _ln(x_ref[...]) * (1.0 + sc2) + sh2
    hb = h.astype(BF16)
    ht = h.T.astype(BF16)
    for cb in range(tt // MM_COLS):
        hb_ref[cb] = ht[:, cb * MM_COLS:(cb + 1) * MM_COLS]
    q = _dot(hb, wq_ref[...])
    lane = lax.broadcasted_iota(jnp.int32, (tt, LANES), 1)
    m_lo = (lane < PEER_KEYS // 2).astype(F32)
    m_hi = 1.0 - m_lo
    av_ref[...] = jnp.full(av_ref.shape, NEG_INF, F32)
    bv_ref[...] = jnp.full(bv_ref.shape, NEG_INF, F32)
    for h in range(PEER_HEADS):
        qh = q[:, h * LANES:(h + 1) * LANES]
        sk = sk_ref[h]
        s0 = _dot_nt(sk, (qh * m_lo).astype(BF16))
        s1 = _dot_nt(sk, (qh * m_hi).astype(BF16))
        ns0, a1, w0, w1 = _route_head(s0, s1, av_ref, bv_ref, cand_ref)
        for tc in range(tt // LANES):
            cols = slice(tc * LANES, (tc + 1) * LANES)
            ns0_ref[h, tc] = ns0[:, cols]
            a1_ref[h, tc] = a1[:, cols]
            w0_ref[h, tc] = w0[:, cols]
            w1_ref[h, tc] = w1[:, cols]


def _route_call(x, mod3, mod_idx, wq, skc, tt):
    t, d = x.shape
    full = lambda shape: pl.BlockSpec(shape, lambda i: (0,) * len(shape))
    rspec = pl.BlockSpec((PEER_HEADS, tt // LANES, PEER_KEYS, LANES), lambda i: (0, i, 0, 0))
    rshape = jax.ShapeDtypeStruct((PEER_HEADS, t // LANES, PEER_KEYS, LANES), F32)
    return pl.pallas_call(
        _route_kernel,
        grid=(t // tt,),
        in_specs=[pl.BlockSpec((tt, d), lambda i: (i, 0)),
                  pl.BlockSpec((1, 1, mod3.shape[2]), lambda i: (mod_idx(i), 0, 0)),
                  full(wq.shape), full(skc.shape)],
        out_specs=[pl.BlockSpec((tt // MM_COLS, d, MM_COLS), lambda i: (i, 0, 0)), rspec, rspec, rspec, rspec],
        out_shape=[jax.ShapeDtypeStruct((t // MM_COLS, d, MM_COLS), BF16), rshape, rshape, rshape, rshape],
        scratch_shapes=[pltpu.VMEM((PAD_TOP, tt), F32), pltpu.VMEM((PAD_TOP, tt), F32),
                        pltpu.VMEM((N_CAND, tt), F32)],
        compiler_params=_cparams(("arbitrary",)),
        name="peer_route",
    )(x, mod3, wq, skc)


MM_COLS = 256
G_ROWS = 32
G_REUSE = 8
PEER_CHUNK = 1024


def _peer_kernel(hb_ref, ns0_ref, a1_ref, w0_ref, w1_ref, u_ref, vt_ref, x_ref, mod_ref, lng_ref, lnb_ref,
                 o_ref, acc_ref, a_ref, ga_ref, *, ni):
    c = pl.program_id(1)
    n_cb = hb_ref.shape[0]
    n_tc = ns0_ref.shape[1]

    @pl.when(c == 0)
    def _():
        acc_ref[...] = jnp.zeros_like(acc_ref)

    for cb in range(n_cb):
        a_blk = _dot(u_ref[...], hb_ref[cb])
        a_ref[2 * cb] = a_blk[:, :LANES]
        a_ref[2 * cb + 1] = a_blk[:, LANES:]

    i_base = pl.multiple_of(c * ni, ni)
    n_jq = PEER_KEYS // G_ROWS

    def gate_column(tc, carry):
        for ig in range(ni // G_REUSE):
            i0 = pl.multiple_of(i_base + ig * G_REUSE, G_REUSE)
            nrows = [ns0_ref[h, tc, pl.ds(i0, G_REUSE), :] for h in range(PEER_HEADS)]
            wrows = [w0_ref[h, tc, pl.ds(i0, G_REUSE), :] for h in range(PEER_HEADS)]
            bn = [[jnp.broadcast_to(nrows[h][r:r + 1], (G_ROWS, LANES)) for r in range(G_REUSE)]
                  for h in range(PEER_HEADS)]
            bw = [[jnp.broadcast_to(wrows[h][r:r + 1], (G_ROWS, LANES)) for r in range(G_REUSE)]
                  for h in range(PEER_HEADS)]
            for jq in range(n_jq):
                jrows = slice(jq * G_ROWS, (jq + 1) * G_ROWS)
                g = [None] * G_REUSE
                for h in range(PEER_HEADS):
                    a1q = a1_ref[h, tc, jrows, :]
                    w1q = w1_ref[h, tc, jrows, :]
                    for r in range(G_REUSE):
                        term = jnp.where(a1q >= bn[h][r], w1q * bw[h][r], 0.0)
                        g[r] = term if g[r] is None else g[r] + term
                for r in range(G_REUSE):
                    rows = slice((ig * G_REUSE + r) * PEER_KEYS + jq * G_ROWS,
                                 (ig * G_REUSE + r) * PEER_KEYS + (jq + 1) * G_ROWS)
                    a = a_ref[tc, rows, :]
                    act = a * (1.0 + lax.erf(a * SQRT_HALF))
                    ga_ref[tc, rows, :] = (g[r] * act).astype(BF16)
        return carry

    lax.fori_loop(0, n_tc, gate_column, 0)

    ga = jnp.concatenate([ga_ref[tc] for tc in range(n_tc)], axis=1)
    acc_ref[...] += _dot(vt_ref[...], ga)

    @pl.when(c == pl.num_programs(1) - 1)
    def _():
        d = D_MODEL
        g2 = mod_ref[0][:, 5 * d:6 * d]
        u = DEEPNORM_ALPHA * x_ref[...] + g2 * acc_ref[...].T
        o_ref[...] = _ln(u) * lng_ref[...] + lnb_ref[...]


def _peer_call(hb, ns0, a1, w0, w1, u_tab, vt_tab, x, mod3, mod_idx, lng, lnb, tt, ec):
    t, d = x.shape
    ne = u_tab.shape[0]
    ni = ec // PEER_KEYS
    full = lambda shape: pl.BlockSpec(shape, lambda i, c: (0,) * len(shape))
    rspec = pl.BlockSpec((PEER_HEADS, tt // LANES, PEER_KEYS, LANES), lambda i, c: (0, i, 0, 0))
    return pl.pallas_call(
        functools.partial(_peer_kernel, ni=ni),
        grid=(t // tt, ne // ec),
        in_specs=[pl.BlockSpec((tt // MM_COLS, d, MM_COLS), lambda i, c: (i, 0, 0)), rspec, rspec, rspec, rspec,
                  pl.BlockSpec((ec, d), lambda i, c: (c, 0)),
                  pl.BlockSpec((d, ec), lambda i, c: (0, c)),
                  pl.BlockSpec((tt, d), lambda i, c: (i, 0)),
                  pl.BlockSpec((1, 1, mod3.shape[2]), lambda i, c: (mod_idx(i), 0, 0)),
                  full(lng.shape), full(lnb.shape)],
        out_specs=pl.BlockSpec((tt, d), lambda i, c: (i, 0)),
        out_shape=jax.ShapeDtypeStruct((t, d), F32),
        scratch_shapes=[pltpu.VMEM((d, tt), F32), pltpu.VMEM((tt // LANES, ec, LANES), F32),
                        pltpu.VMEM((tt // LANES, ec, LANES), BF16)],
        compiler_params=_cparams(("arbitrary", "arbitrary")),
        name="peer_experts",
    )(hb, ns0, a1, w0, w1, u_tab, vt_tab, x, mod3, lng, lnb)


TILE_IN_PROJ = 256
TILE_TOKENS = 512
TILE_ROUTE = 256
TILE_FOURIER = 512
TILE_Q_LONG = 128
TILE_Q_SHORT = 256
LONG_KEYS = 1024


def _pick_tile(n, want):
    t = min(n, want)
    while n % t:
        t //= 2
    return t


def _rope_tables(seq):
    n_rows = seq // GRID_W
    row = jnp.repeat(jnp.arange(n_rows), GRID_W).astype(F32)
    col = jnp.tile(jnp.arange(GRID_W), n_rows).astype(F32)
    n_freq = DIFF_HEAD_DIM // 4
    freqs = ROPE_THETA ** (-jnp.arange(n_freq, dtype=F32) / n_freq)
    ang = jnp.concatenate([row[:, None] * freqs, col[:, None] * freqs], -1)
    cos = jnp.repeat(jnp.cos(ang), 2, axis=-1)
    sin = jnp.repeat(jnp.sin(ang), 2, axis=-1)
    sign = jnp.where(jnp.arange(DIFF_HEAD_DIM) % 2 == 0, -1.0, 1.0).astype(F32)
    reps = LANES // DIFF_HEAD_DIM
    return jnp.tile(cos, (1, reps)), jnp.tile(sin * sign, (1, reps))


def _dup_heads(a):
    b, s, _ = a.shape
    a4 = a.reshape(b, s, GQA_KV_HEADS, 1, GQA_HEAD_DIM)
    return jnp.broadcast_to(a4, (b, s, GQA_KV_HEADS, 2, GQA_HEAD_DIM)).reshape(b, s, 4 * GQA_HEAD_DIM)


def _values_with_ones(a):
    b, s, _ = a.shape
    a4 = a.reshape(b, s, GQA_KV_HEADS, 1, GQA_HEAD_DIM)
    return jnp.concatenate([a4, jnp.ones_like(a4)], axis=3).reshape(b, s, 4 * GQA_HEAD_DIM)


def _layer(x, batch, seq, mod3, mod_idx_fn, lw, rope_tabs, ctx_kv, dft, emit_kv):
    t = batch * seq
    tile = lambda want: _pick_tile(seq if rope_tabs is not None else t, want)
    tt = tile(TILE_IN_PROJ)
    mod_idx = lambda i: mod_idx_fn(i, max(seq // tt, 1))
    outs = _in_proj_call(x, mod3, mod_idx, lw["w_in"], lw["qg"], lw["kg"], lw["bd"], rope_tabs, seq, tt, emit_kv)
    dq, dk, dv, gq, gk, gv, fz, gates = outs[:8]
    r3 = lambda a: a.reshape(batch, seq, a.shape[-1])
    k_att, v_att, gk_att, gv_att = r3(dk), r3(dv), r3(gk), r3(gv)
    if ctx_kv is not None:
        cdk, cdv, cgk, cgv = ctx_kv
        k_att = jnp.concatenate([k_att, cdk], axis=1)
        v_att = jnp.concatenate([v_att, cdv], axis=1)
        gk_att = jnp.concatenate([gk_att, cgk], axis=1)
        gv_att = jnp.concatenate([gv_att, cgv], axis=1)
    tq = _pick_tile(seq, TILE_Q_LONG if k_att.shape[1] > LONG_KEYS else TILE_Q_SHORT)
    d_o, g_o = _attn_call(r3(dq), k_att, v_att, r3(gq), _dup_heads(gk_att), _values_with_ones(gv_att),
                          lw["diff_lam"], lw["dng"], lw["lam_init"], tq)
    f_o = _fourier_call(r3(fz), dft[0], dft[1], lw["cc"], lw["sc"], _pick_tile(seq, TILE_FOURIER))
    flat = lambda a: a.reshape(t, a.shape[-1])
    tm = tile(TILE_TOKENS)
    mod_idx_m = lambda i: mod_idx_fn(i, max(seq // tm, 1))
    x1 = _merge_call(flat(d_o), flat(g_o), flat(f_o), gates, x, mod3, mod_idx_m, lw["wb"], lw["wo"],
                     lw["lng0"], lw["lnb0"], tm)
    tr = tile(TILE_ROUTE)
    mod_idx_r = lambda i: mod_idx_fn(i, max(seq // tr, 1))
    hb, ns0, a1, w0, w1 = _route_call(x1, mod3, mod_idx_r, lw["wq"], lw["skc"], tr)
    x2 = _peer_call(hb, ns0, a1, w0, w1, lw["u"], lw["vt"], x1, mod3, mod_idx_m, lw["lng1"], lw["lnb1"], tm, PEER_CHUNK)
    return x2, outs[8:]


def kernel(x_prompt, x_sample, cache_diff_k, cache_diff_v, cache_gqa_k, cache_gqa_v, c, c_ctx, w_mod, b_mod, w_in,
           diff_lam, diff_norm_g, q_norm_g, k_norm_g, w_branch, w_out, ln_g, ln_b, peer_wq, peer_subkeys, peer_u,
           peer_v):
    depth = w_in.shape[0]
    batch, seq, d = x_prompt.shape
    dbatch, dseq, _ = x_sample.shape
    past = cache_diff_k.shape[2]

    rows = ((dbatch + 1 + 7) // 8) * 8
    cs = jnp.zeros((rows, d), F32).at[:dbatch].set(c).at[dbatch].set(c_ctx)
    mod_all = _mod_call(cs, w_mod, b_mod)

    w_in_b = w_in.astype(BF16)
    wb_b = w_branch.astype(BF16)
    wo_b = w_out.astype(BF16)
    wq_b = peer_wq.astype(BF16)
    u_b = peer_u.astype(BF16)
    vt_b = jnp.swapaxes(peer_v, 1, 2).astype(BF16)
    skc = jnp.transpose(peer_subkeys, (0, 1, 3, 2, 4)).reshape(depth, PEER_HEADS, PEER_KEYS, PEER_KEYS).astype(BF16)
    bd = jnp.asarray(np.kron(np.eye(8, dtype=np.float32), np.full((64, 64), 1.0 / 64, np.float32)), BF16)
    ck, sk_ = _dft_tables(FOURIER_GROUP_DIM)
    eye_g = jnp.eye(FOURIER_GROUPS, dtype=F32)
    cc = jnp.kron(eye_g, ck).astype(BF16)
    sc = jnp.kron(eye_g, sk_).astype(BF16)
    dft_ctx = tuple(a.astype(BF16) for a in _dft_tables(seq))
    dft_lat = tuple(a.astype(BF16) for a in _dft_tables(dseq))
    rope_tabs = _rope_tables(dseq)

    def layer_weights(l):
        return {"w_in": w_in_b[l], "qg": jnp.tile(q_norm_g[l], 8)[None, :], "kg": jnp.tile(k_norm_g[l], 2)[None, :],
                "bd": bd, "diff_lam": diff_lam[l], "dng": diff_norm_g[l][None, :],
                "lam_init": 0.8 - 0.6 * math.exp(-0.3 * l), "cc": cc, "sc": sc, "wb": wb_b[l], "wo": wo_b[l],
                "lng0": ln_g[l, 0][None, :], "lnb0": ln_b[l, 0][None, :], "lng1": ln_g[l, 1][None, :],
                "lnb1": ln_b[l, 1][None, :], "wq": wq_b[l], "skc": skc[l], "u": u_b[l], "vt": vt_b[l]}

    xp = x_prompt.reshape(batch * seq, d)
    ctx_idx = lambda i, bps: dbatch
    kv = [[], [], [], []]
    for l in range(depth):
        mod3 = mod_all[l].reshape(rows, 1, 6 * d)
        xp, own = _layer(xp, batch, seq, mod3, ctx_idx, layer_weights(l), None, None, dft_ctx, True)
        for lst, a in zip(kv, own):
            lst.append(a)
    new_diff_k = jnp.stack(kv[0], 0).reshape(depth, batch, seq, DIFF_HEADS, 2 * DIFF_HEAD_DIM).swapaxes(0, 1)
    new_diff_v = jnp.stack(kv[1], 0).reshape(depth, batch, seq, DIFF_HEADS, 2 * DIFF_HEAD_DIM).swapaxes(0, 1)
    new_gqa_k = jnp.stack(kv[2], 0).reshape(depth, batch, seq, GQA_KV_HEADS, GQA_HEAD_DIM).swapaxes(0, 1)
    new_gqa_v = jnp.stack(kv[3], 0).reshape(depth, batch, seq, GQA_KV_HEADS, GQA_HEAD_DIM).swapaxes(0, 1)

    xs = x_sample.reshape(dbatch * dseq, d)
    lat_idx = lambda i, bps: i // bps
    for l in range(depth):
        mod3 = mod_all[l].reshape(rows, 1, 6 * d)
        ctx_kv = (cache_diff_k[:, l].reshape(dbatch, past, 512).astype(BF16),
                  cache_diff_v[:, l].reshape(dbatch, past, 512).astype(BF16),
                  cache_gqa_k[:, l].reshape(dbatch, past, 128).astype(BF16),
                  cache_gqa_v[:, l].reshape(dbatch, past, 128).astype(BF16))
        xs, _ = _layer(xs, dbatch, dseq, mod3, lat_idx, layer_weights(l), rope_tabs, ctx_kv, dft_lat, False)

    return (xp.reshape(batch, seq, d), xs.reshape(dbatch, dseq, d), new_diff_k, new_diff_v, new_gqa_k, new_gqa_v)
```
